```python
import math
import functools
import jax
import jax.numpy as jnp
from jax import lax
import numpy as np

D_MODEL = 1024
BATCH = 8
SEQ = 4096
DEPTH = 2
DEC_BATCH = 32
DEC_SEQ = 4
PAST_LEN = 16384
PAGE_SIZE = 128

N_A_LAYERS = DEPTH // 2
N_B_LAYERS = DEPTH - N_A_LAYERS

SSM_EXPAND = 2
D_INNER = SSM_EXPAND * D_MODEL
SSM_HEAD_DIM = 64
SSM_HEADS = D_INNER // SSM_HEAD_DIM
SSM_GROUPS = 4
D_STATE = 128
CONV_W = 4
CONV_DIM = D_INNER + 2 * SSM_GROUPS * D_STATE
SSM_IN_DIM = D_INNER + CONV_DIM + SSM_HEADS
SSD_CHUNK = 128
GATED_NORM_EPS = 1e-5

HALF_DIM = 64
ATTN_HEADS = D_MODEL // (2 * HALF_DIM)
QK_DIM = 2 * HALF_DIM
V_DIM = 2 * HALF_DIM
Q_BLOCK = 128
ATTN_SCALE = HALF_DIM ** -0.5

N_EGROUPS = 4
EXPERTS_PER_GROUP = 8
N_EXPERTS = N_EGROUPS * EXPERTS_PER_GROUP
TOP_K = 2
EXPERT_FF = D_MODEL // 2
MOE_BLOCK = 128

RMS_EPS = 1e-6

kernel_name = 'yoco_mamba2_diffattn_hmoe_step'


def _rmsnorm(x, g, eps=RMS_EPS):
    xf = x.astype(jnp.float32)
    y = xf * lax.rsqrt(jnp.mean(xf * xf, axis=-1, keepdims=True) + eps)
    return (y * g.astype(jnp.float32)).astype(x.dtype)


def _ssd_scan(x, dt, a, bm, cm, h0):
    b, L, H, P = x.shape
    G, N = bm.shape[2], bm.shape[3]
    R = H // G
    T = SSD_CHUNK if L % SSD_CHUNK == 0 else L
    C = L // T
    f32 = jnp.float32
    la = (dt * a).reshape(b, C, T, G, R)
    xdt = (x.astype(f32) * dt[..., None]).reshape(b, C, T, G, R, P)
    bc = bm.astype(f32).reshape(b, C, T, G, N)
    cc = cm.astype(f32).reshape(b, C, T, G, N)
    cs = jnp.cumsum(la, axis=2)
    causal = jnp.tril(jnp.ones((T, T), dtype=bool))
    seg = cs[:, :, :, None] - cs[:, :, None, :]
    decay = jnp.exp(jnp.where(causal[:, :, None, None], seg, -jnp.inf))
    cb = jnp.einsum('bctgn,bcsgn->bctsg', cc, bc)
    y_diag = jnp.einsum('bctsgr,bcsgrp->bctgrp', cb[..., None] * decay, xdt)
    to_end = jnp.exp(cs[:, :, -1:] - cs)
    chunk_states = jnp.einsum('bctgn,bctgr,bctgrp->bcgrpn', bc, to_end, xdt)
    chunk_decay = jnp.exp(cs[:, :, -1])

    def step(state, inp):
        s_c, d_c = inp
        return state * d_c[..., None, None] + s_c, state

    h_init = h0.astype(f32).reshape(b, G, R, P, N)
    h_final, h_in = lax.scan(step, h_init, (jnp.moveaxis(chunk_states, 1, 0), jnp.moveaxis(chunk_decay, 1, 0)))
    h_in = jnp.moveaxis(h_in, 0, 1)
    y_off = jnp.einsum('bctgn,bcgrpn,bctgr->bctgrp', cc, h_in, jnp.exp(cs))
    y = (y_diag + y_off).reshape(b, L, H, P)
    return y, h_final.reshape(b, H, P, N)


def _mamba2_mixer(h, conv_buf, ssm_state, w_in, conv_w, conv_b, dt_bias, a_log, d_skip, gate_norm, w_out):
    b, L, _ = h.shape
    f32 = jnp.float32
    zxbcdt = h @ w_in
    z = zxbcdt[..., :D_INNER]
    xbc = zxbcdt[..., D_INNER:D_INNER + CONV_DIM]
    dt_raw = zxbcdt[..., D_INNER + CONV_DIM:]
    xpad = jnp.concatenate([conv_buf.astype(xbc.dtype), xbc], axis=1)
    new_conv = xpad[:, L:]
    conv = conv_b
    for k in range(CONV_W):
        conv = conv + xpad[:, k:k + L] * conv_w[k]
    xbc = jax.nn.silu(conv)
    xs = xbc[..., :D_INNER].reshape(b, L, SSM_HEADS, SSM_HEAD_DIM)
    bm = xbc[..., D_INNER:D_INNER + SSM_GROUPS * D_STATE].reshape(b, L, SSM_GROUPS, D_STATE)
    cm = xbc[..., D_INNER + SSM_GROUPS * D_STATE:].reshape(b, L, SSM_GROUPS, D_STATE)
    dt = jax.nn.softplus(dt_raw.astype(f32) + dt_bias.astype(f32))
    a = -jnp.exp(a_log.astype(f32))
    y, new_state = _ssd_scan(xs, dt, a, bm, cm, ssm_state)
    y = y + xs.astype(f32) * d_skip.astype(f32)[:, None]
    gated = (y.reshape(b, L, D_INNER) * jax.nn.silu(z.astype(f32))).reshape(b, L, SSM_GROUPS, D_INNER // SSM_GROUPS)
    gated = gated * lax.rsqrt(jnp.mean(gated * gated, axis=-1, keepdims=True) + GATED_NORM_EPS)
    gated = gated.reshape(b, L, D_INNER) * gate_norm.astype(f32)
    out = gated.astype(h.dtype) @ w_out
    return out, new_conv, new_state.astype(ssm_state.dtype)


def _grouped_experts(t, expert_idx, gates, w_gate, w_up, w_down):
    T, D = t.shape
    K = expert_idx.shape[1]
    E = w_gate.shape[0]
    A = T * K
    flat_e = expert_idx.reshape(A)
    flat_tok = jnp.arange(A, dtype=jnp.int32) // K
    flat_w = gates.reshape(A)
    order = jnp.argsort(flat_e)
    se, stok, sw = flat_e[order], flat_tok[order], flat_w[order]
    counts = jnp.bincount(flat_e, length=E)
    padded = (counts + MOE_BLOCK - 1) // MOE_BLOCK * MOE_BLOCK
    pad_end = jnp.cumsum(padded)
    pad_start = pad_end - padded
    start = jnp.cumsum(counts) - counts
    dest = pad_start[se] + jnp.arange(A, dtype=jnp.int32) - start[se]
    n_blocks = (A + E * (MOE_BLOCK - 1) + MOE_BLOCK - 1) // MOE_BLOCK
    n_slots = n_blocks * MOE_BLOCK
    slot_tok = jnp.full((n_slots,), T, dtype=jnp.int32).at[dest].set(stok)
    block_expert = jnp.minimum(jnp.searchsorted(pad_end, jnp.arange(n_blocks) * MOE_BLOCK, side='right'), E - 1)
    t_pad = jnp.concatenate([t, jnp.zeros((1, D), t.dtype)], axis=0)
    xb = t_pad[slot_tok].reshape(n_blocks, MOE_BLOCK, D)

    def run(args):
        xblk, e = args
        return (jax.nn.silu(xblk @ w_gate[e]) * (xblk @ w_up[e])) @ w_down[e]

    yb = lax.map(run, (xb, block_expert)).reshape(n_slots, D)
    contrib = yb[dest] * sw[:, None].astype(yb.dtype)
    return jnp.zeros((T, D), yb.dtype).at[stok].add(contrib)


def _hier_moe(h, w_group, w_expert, w_gate, w_up, w_down):
    b, L, d = h.shape
    T = b * L
    t = h.reshape(T, d)
    rows = jnp.arange(T)
    g_logits = (t @ w_group).astype(jnp.float32)
    g_sel = jnp.argmax(g_logits, axis=-1)
    p_group = jax.nn.softmax(g_logits, axis=-1)[rows, g_sel]
    e_logits = (t @ w_expert).astype(jnp.float32).reshape(T, N_EGROUPS, EXPERTS_PER_GROUP)
    e_in_group = e_logits[rows, g_sel]
    top_val, top_idx = lax.top_k(e_in_group, TOP_K)
    gates = p_group[:, None] * jax.nn.softmax(top_val, axis=-1)
    expert_idx = (g_sel[:, None] * EXPERTS_PER_GROUP + top_idx).astype(jnp.int32)
    out = _grouped_experts(t, expert_idx, gates, w_gate, w_up, w_down)
    return out.reshape(b, L, d).astype(h.dtype)


def _shared_kv(x, kv_norm, w_k, w_v):
    s = _rmsnorm(x, kv_norm)
    b, L, _ = x.shape
    return (s @ w_k).reshape(b, L, ATTN_HEADS, QK_DIM), (s @ w_v).reshape(b, L, ATTN_HEADS, V_DIM)


def _lambda_init(layer_idx):
    return 0.8 - 0.6 * math.exp(-0.3 * layer_idx)


def _diff_lambda(lam_qk, lam_init):
    lq = lam_qk.astype(jnp.float32)
    return jnp.exp(jnp.sum(lq[0] * lq[1])) - jnp.exp(jnp.sum(lq[2] * lq[3])) + lam_init


def _diff_out(o, subln, w_o, lam_init, dtype):
    b, L = o.shape[0], o.shape[1]
    o = o * lax.rsqrt(jnp.mean(o * o, axis=-1, keepdims=True) + RMS_EPS)
    o = o * subln.astype(jnp.float32) * (1.0 - lam_init)
    return o.reshape(b, L, ATTN_HEADS * V_DIM).astype(dtype) @ w_o


def _diff_attn_prompt(h, k, v, w_q, lam_qk, subln, w_o, lam_init):
    b, S, _ = h.shape
    f32 = jnp.float32
    lam = _diff_lambda(lam_qk, lam_init)
    nq = S // Q_BLOCK
    q = (h @ w_q).astype(f32).reshape(b, nq, Q_BLOCK, ATTN_HEADS, 2, HALF_DIM)
    q = jnp.transpose(q, (1, 0, 3, 4, 2, 5))
    kt = k.astype(f32).reshape(b, S, ATTN_HEADS, 2, HALF_DIM).transpose(0, 2, 3, 1, 4)
    vt = v.astype(f32).transpose(0, 2, 1, 3)
    kpos = jnp.arange(S)

    def block(args):
        qb, q0 = args
        s = jnp.einsum('bhmqd,bhmkd->bhmqk', qb, kt) * ATTN_SCALE
        visible = kpos[None, :] <= (q0 + jnp.arange(Q_BLOCK))[:, None]
        p = jax.nn.softmax(jnp.where(visible, s, -jnp.inf), axis=-1)
        w = p[:, :, 0] - lam * p[:, :, 1]
        return jnp.einsum('bhqk,bhkv->bhqv', w, vt)

    o = lax.map(block, (q, jnp.arange(nq) * Q_BLOCK))
    o = jnp.transpose(o, (1, 0, 3, 2, 4)).reshape(b, S, ATTN_HEADS, V_DIM)
    return _diff_out(o, subln, w_o, lam_init, h.dtype)


def _diff_attn_sample(h, k_new, v_new, w_q, lam_qk, subln, w_o, lam_init, cache_k, cache_v, page_table):
    b, Tq, _ = h.shape
    f32 = jnp.float32
    lam = _diff_lambda(lam_qk, lam_init)
    q = (h @ w_q).astype(f32).reshape(b, Tq, ATTN_HEADS, 2, HALF_DIM).transpose(0, 2, 3, 1, 4)

    def attend(carry, kb, vb, mask):
        m, l, acc = carry
        s = jnp.einsum('bhmqd,bhmkd->bhmqk', q, kb) * ATTN_SCALE
        if mask is not None:
            s = jnp.where(mask, s, -jnp.inf)
        m_new = jnp.maximum(m, jnp.max(s, axis=-1))
        p = jnp.exp(s - m_new[..., None])
        corr = jnp.exp(m - m_new)
        acc = acc * corr[..., None] + jnp.einsum('bhmqk,bhkv->bhmqv', p, vb)
        return (m_new, l * corr + jnp.sum(p, axis=-1), acc)

    def page_step(carry, phys):
        kb = cache_k[phys].astype(f32).reshape(b, PAGE_SIZE, ATTN_HEADS, 2, HALF_DIM).transpose(0, 2, 3, 1, 4)
        vb = cache_v[phys].astype(f32).transpose(0, 2, 1, 3)
        return attend(carry, kb, vb, None), None

    init = (jnp.full((b, ATTN_HEADS, 2, Tq), -jnp.inf, f32),
            jnp.zeros((b, ATTN_HEADS, 2, Tq), f32),
            jnp.zeros((b, ATTN_HEADS, 2, Tq, V_DIM), f32))
    carry, _ = lax.scan(page_step, init, jnp.transpose(page_table))
    kn = k_new.astype(f32).reshape(b, Tq, ATTN_HEADS, 2, HALF_DIM).transpose(0, 2, 3, 1, 4)
    vn = v_new.astype(f32).transpose(0, 2, 1, 3)
    m, l, acc = attend(carry, kn, vn, jnp.tril(jnp.ones((Tq, Tq), dtype=bool)))
    o = acc[:, :, 0] / l[:, :, 0, :, None] - lam * (acc[:, :, 1] / l[:, :, 1, :, None])
    o = jnp.transpose(o, (0, 2, 1, 3))
    return _diff_out(o, subln, w_o, lam_init, h.dtype)


def _trunk(x, conv_bufs, ssm_states, attention, p):
    new_conv, new_ssm = [], []
    k = v = None
    for i in range(DEPTH):
        if i < N_A_LAYERS:
            j = i
            mix, cb, st = _mamba2_mixer(_rmsnorm(x, p['ssm_norm'][j]), conv_bufs[j], ssm_states[j],
                                        p['ssm_w_in'][j], p['ssm_conv_w'][j], p['ssm_conv_b'][j],
                                        p['ssm_dt_bias'][j], p['ssm_a_log'][j], p['ssm_d'][j],
                                        p['ssm_gate_norm'][j], p['ssm_w_out'][j])
            new_conv.append(cb)
            new_ssm.append(st)
        else:
            j = i - N_A_LAYERS
            mix = attention(_rmsnorm(x, p['attn_norm'][j]), k, v, p['attn_w_q'][j], p['attn_lambda'][j],
                            p['attn_subln'][j], p['attn_w_o'][j], _lambda_init(i))
        x = x + mix.astype(x.dtype)
        x = x + _hier_moe(_rmsnorm(x, p['moe_norm'][i]), p['moe_w_group'][i], p['moe_w_expert'][i],
                          p['moe_w_gate'][i], p['moe_w_up'][i], p['moe_w_down'][i])
        if i == N_A_LAYERS - 1:
            k, v = _shared_kv(x, p['kv_norm'], p['w_k'], p['w_v'])
    y = _rmsnorm(x, p['final_norm'])
    return y, jnp.stack(new_conv), jnp.stack(new_ssm), k, v


def setup_inputs(seed: int = 0) -> dict:
    key = jax.random.key(seed)
    ks = jax.random.split(key, 40)
    f32 = jnp.float32
    n_pages = PAST_LEN // PAGE_SIZE
    n_pool = (DEC_BATCH * n_pages * 5) // 4

    def nrm(k, shape, scale):
        return jax.random.normal(k, shape, f32) * scale

    def gain(k, shape):
        return 1.0 + 0.02 * jax.random.normal(k, shape, f32)

    x_prompt = nrm(ks[0], (BATCH, SEQ, D_MODEL), 1.0)
    x_sample = nrm(ks[1], (DEC_BATCH, DEC_SEQ, D_MODEL), 1.0)
    state_ssm = nrm(ks[2], (N_A_LAYERS, DEC_BATCH, SSM_HEADS, SSM_HEAD_DIM, D_STATE), 0.1)
    state_conv = nrm(ks[3], (N_A_LAYERS, DEC_BATCH, CONV_W - 1, CONV_DIM), 1.0)
    cache_k = nrm(ks[4], (n_pool, PAGE_SIZE, ATTN_HEADS, QK_DIM), 1.0)
    cache_v = nrm(ks[5], (n_pool, PAGE_SIZE, ATTN_HEADS, V_DIM), 1.0)
    page_table = jax.random.permutation(ks[6], n_pool)[:DEC_BATCH * n_pages].reshape(DEC_BATCH, n_pages).astype(jnp.int32)

    ssm_norm = gain(ks[7], (N_A_LAYERS, D_MODEL))
    ssm_w_in = nrm(ks[8], (N_A_LAYERS, D_MODEL, SSM_IN_DIM), D_MODEL ** -0.5)
    ssm_conv_w = nrm(ks[9], (N_A_LAYERS, CONV_W, CONV_DIM), CONV_W ** -0.5)
    ssm_conv_b = nrm(ks[10], (N_A_LAYERS, CONV_DIM), 0.02)
    dt0 = jnp.exp(jax.random.uniform(ks[11], (N_A_LAYERS, SSM_HEADS), f32, math.log(1e-3), math.log(1e-1)))
    ssm_dt_bias = dt0 + jnp.log(-jnp.expm1(-dt0))
    ssm_a_log = jnp.log(jax.random.uniform(ks[12], (N_A_LAYERS, SSM_HEADS), f32, 1.0, 16.0))
    ssm_d = gain(ks[13], (N_A_LAYERS, SSM_HEADS))
    ssm_gate_norm = gain(ks[14], (N_A_LAYERS, D_INNER))
    ssm_w_out = nrm(ks[15], (N_A_LAYERS, D_INNER, D_MODEL), D_INNER ** -0.5)

    kv_norm = gain(ks[16], (D_MODEL,))
    w_k = nrm(ks[17], (D_MODEL, ATTN_HEADS * QK_DIM), D_MODEL ** -0.5)
    w_v = nrm(ks[18], (D_MODEL, ATTN_HEADS * V_DIM), D_MODEL ** -0.5)

    attn_norm = gain(ks[19], (N_B_LAYERS, D_MODEL))
    attn_w_q = nrm(ks[20], (N_B_LAYERS, D_MODEL, ATTN_HEADS * QK_DIM), D_MODEL ** -0.5)
    attn_lambda = nrm(ks[21], (N_B_LAYERS, 4, HALF_DIM), 0.1)
    attn_subln = gain(ks[22], (N_B_LAYERS, V_DIM))
    attn_w_o = nrm(ks[23], (N_B_LAYERS, ATTN_HEADS * V_DIM, D_MODEL), (ATTN_HEADS * V_DIM) ** -0.5)

    moe_norm = gain(ks[24], (DEPTH, D_MODEL))
    moe_w_group = nrm(ks[25], (DEPTH, D_MODEL, N_EGROUPS), D_MODEL ** -0.5)
    moe_w_expert = nrm(ks[26], (DEPTH, D_MODEL, N_EXPERTS), D_MODEL ** -0.5)
    moe_w_gate = nrm(ks[27], (DEPTH, N_EXPERTS, D_MODEL, EXPERT_FF), D_MODEL ** -0.5)
    moe_w_up = nrm(ks[28], (DEPTH, N_EXPERTS, D_MODEL, EXPERT_FF), D_MODEL ** -0.5)
    moe_w_down = nrm(ks[29], (DEPTH, N_EXPERTS, EXPERT_FF, D_MODEL), EXPERT_FF ** -0.5)
    final_norm = gain(ks[30], (D_MODEL,))

    return {'x_prompt': x_prompt, 'x_sample': x_sample, 'state_ssm': state_ssm, 'state_conv': state_conv,
            'cache_k': cache_k, 'cache_v': cache_v, 'page_table': page_table,
            'ssm_norm': ssm_norm, 'ssm_w_in': ssm_w_in, 'ssm_conv_w': ssm_conv_w, 'ssm_conv_b': ssm_conv_b,
            'ssm_dt_bias': ssm_dt_bias, 'ssm_a_log': ssm_a_log, 'ssm_d': ssm_d, 'ssm_gate_norm': ssm_gate_norm,
            'ssm_w_out': ssm_w_out, 'kv_norm': kv_norm, 'w_k': w_k, 'w_v': w_v,
            'attn_norm': attn_norm, 'attn_w_q': attn_w_q, 'attn_lambda': attn_lambda, 'attn_subln': attn_subln,
            'attn_w_o': attn_w_o, 'moe_norm': moe_norm, 'moe_w_group': moe_w_group, 'moe_w_expert': moe_w_expert,
            'moe_w_gate': moe_w_gate, 'moe_w_up': moe_w_up, 'moe_w_down': moe_w_down, 'final_norm': final_norm}


def reference(x_prompt, x_sample, state_ssm, state_conv, cache_k, cache_v, page_table,
              ssm_norm, ssm_w_in, ssm_conv_w, ssm_conv_b, ssm_dt_bias, ssm_a_log, ssm_d, ssm_gate_norm,
              ssm_w_out, kv_norm, w_k, w_v, attn_norm, attn_w_q, attn_lambda, attn_subln, attn_w_o,
              moe_norm, moe_w_group, moe_w_expert, moe_w_gate, moe_w_up, moe_w_down, final_norm):
    params = dict(ssm_norm=ssm_norm, ssm_w_in=ssm_w_in, ssm_conv_w=ssm_conv_w, ssm_conv_b=ssm_conv_b,
                  ssm_dt_bias=ssm_dt_bias, ssm_a_log=ssm_a_log, ssm_d=ssm_d, ssm_gate_norm=ssm_gate_norm,
                  ssm_w_out=ssm_w_out, kv_norm=kv_norm, w_k=w_k, w_v=w_v, attn_norm=attn_norm,
                  attn_w_q=attn_w_q, attn_lambda=attn_lambda, attn_subln=attn_subln, attn_w_o=attn_w_o,
                  moe_norm=moe_norm, moe_w_group=moe_w_group, moe_w_expert=moe_w_expert,
                  moe_w_gate=moe_w_gate, moe_w_up=moe_w_up, moe_w_down=moe_w_down, final_norm=final_norm)
    bp = x_prompt.shape[0]
    conv0 = jnp.zeros((N_A_LAYERS, bp, CONV_W - 1, CONV_DIM), x_prompt.dtype)
    ssm0 = jnp.zeros((N_A_LAYERS, bp, SSM_HEADS, SSM_HEAD_DIM, D_STATE), x_prompt.dtype)
    y_prompt, p_conv, p_ssm, p_k, p_v = _trunk(x_prompt, conv0, ssm0, _diff_attn_prompt, params)
    sample_attention = functools.partial(_diff_attn_sample, cache_k=cache_k, cache_v=cache_v, page_table=page_table)
    y_sample, s_conv, s_ssm, s_k, s_v = _trunk(x_sample, state_conv, state_ssm, sample_attention, params)
    return (y_prompt, y_sample, p_ssm, p_conv, p_k, p_v, s_ssm, s_conv, s_k, s_v)
```

```python
import functools
import math

import jax
import jax.numpy as jnp
from jax import lax
from jax.experimental import pallas as pl
from jax.experimental.pallas import tpu as pltpu

F32 = jnp.float32
BF16 = jnp.bfloat16

D_MODEL = 1024
DEPTH = 2
N_A_LAYERS = DEPTH // 2
D_INNER = 2 * D_MODEL
SSM_HEAD_DIM = 64
SSM_HEADS = D_INNER // SSM_HEAD_DIM
SSM_GROUPS = 4
HEADS_PER_GROUP = SSM_HEADS // SSM_GROUPS
D_STATE = 128
CONV_W = 4
CONV_DIM = D_INNER + 2 * SSM_GROUPS * D_STATE
ZX_DIM = D_INNER + CONV_DIM
SSD_CHUNK = 128
GATED_NORM_EPS = 1e-5
HALF_DIM = 64
ATTN_HEADS = D_MODEL // (2 * HALF_DIM)
HEAD_W = 2 * HALF_DIM
ATTN_SCALE = HALF_DIM ** -0.5
N_EGROUPS = 4
EXPERTS_PER_GROUP = 8
N_EXPERTS = N_EGROUPS * EXPERTS_PER_GROUP
TOP_K = 2
EXPERT_FF = D_MODEL // 2
RMS_EPS = 1e-6
PAGE_SIZE = 128

LANES = 128
SUBLANES = 8
MOE_ROWS = 256
ATTN_TILE = 256
PAGES_PER_STEP = 4
NEG_BIG = -1e30
VMEM_LIMIT = 56 * 1024 * 1024


def _params(*sem):
    return pltpu.CompilerParams(dimension_semantics=sem, vmem_limit_bytes=VMEM_LIMIT)


def _silu(x):
    return x * jax.nn.sigmoid(x)


def _split3(x):
    hi = x.astype(BF16)
    r1 = x - hi.astype(F32)
    mid = r1.astype(BF16)
    lo = (r1 - mid.astype(F32)).astype(BF16)
    return jnp.concatenate([hi, mid, lo], axis=1)


def _norm_matmul_kernel(*refs, n_w, has_small):
    x_ref, g_ref = refs[0], refs[1]
    w_refs = refs[2:2 + n_w]
    pos = 2 + n_w
    ws_ref = refs[pos] if has_small else None
    pos += int(has_small)
    o_refs = refs[pos:pos + n_w]
    pos += n_w
    os_ref = refs[pos] if has_small else None
    pos += int(has_small)
    h_scr = refs[pos]

    @pl.when(pl.program_id(1) == 0)
    def _():
        x = x_ref[...]
        ms = jnp.mean(x * x, axis=-1, keepdims=True)
        h = (x * lax.rsqrt(ms + RMS_EPS) * g_ref[...]).astype(BF16)
        h_scr[...] = h
        if has_small:
            os_ref[...] = jnp.dot(h, ws_ref[...], preferred_element_type=F32)

    h = h_scr[...]
    for w_ref, o_ref in zip(w_refs, o_refs):
        o_ref[...] = jnp.dot(h, w_ref[...], preferred_element_type=F32).astype(o_ref.dtype)


def _norm_matmul(x, gain, weights, w_small=None, tm=512, tn=512):
    t, d = x.shape
    n = weights[0].shape[1]
    tm = min(tm, t)
    tn = min(tn, n)
    n_w = len(weights)
    has_small = w_small is not None
    in_specs = [pl.BlockSpec((tm, d), lambda i, j: (i, 0)),
                pl.BlockSpec((1, d), lambda i, j: (0, 0))]
    in_specs += [pl.BlockSpec((d, tn), lambda i, j: (0, j))] * n_w
    out_shape = [jax.ShapeDtypeStruct((t, n), F32)] * n_w
    out_specs = [pl.BlockSpec((tm, tn), lambda i, j: (i, j))] * n_w
    args = [x, gain.reshape(1, d)] + list(weights)
    if has_small:
        in_specs.append(pl.BlockSpec((d, LANES), lambda i, j: (0, 0)))
        out_shape.append(jax.ShapeDtypeStruct((t, LANES), F32))
        out_specs.append(pl.BlockSpec((tm, LANES), lambda i, j: (i, 0)))
        args.append(w_small)
    return pl.pallas_call(
        functools.partial(_norm_matmul_kernel, n_w=n_w, has_small=has_small),
        grid=(t // tm, n // tn),
        in_specs=in_specs, out_specs=out_specs, out_shape=out_shape,
        scratch_shapes=[pltpu.VMEM((tm, d), BF16)],
        compiler_params=_params("parallel", "arbitrary"),
        name="norm_matmul",
    )(*args)


def _matmul_res_kernel(a_ref, w_ref, r_ref, o_ref):
    o_ref[...] = r_ref[...] + jnp.dot(a_ref[...], w_ref[...], preferred_element_type=F32)


def _matmul_res(a, w, res, tm=512, tn=512):
    t, k = a.shape
    n = w.shape[1]
    tm = min(tm, t)
    return pl.pallas_call(
        _matmul_res_kernel,
        grid=(t // tm, n // tn),
        in_specs=[pl.BlockSpec((tm, k), lambda i, j: (i, 0)),
                  pl.BlockSpec((k, tn), lambda i, j: (0, j)),
                  pl.BlockSpec((tm, tn), lambda i, j: (i, j))],
        out_specs=pl.BlockSpec((tm, tn), lambda i, j: (i, j)),
        out_shape=jax.ShapeDtypeStruct((t, n), F32),
        compiler_params=_params("parallel", "arbitrary"),
        name="matmul_res",
    )(a, w, res)


def _ssd_kernel(zx_ref, dt_ref, cbuf_ref, h0_ref, convw_ref, convb_ref, dtb_ref, alog_ref, dfull_ref,
                gnorm_ref, tri_ref, e_ref, e2_ref, g_ref, ht_ref, state_scr, tail_scr, *, tc, valid):
    ci = pl.program_id(1)
    nc = pl.num_programs(1)
    gw = HEADS_PER_GROUP * SSM_HEAD_DIM
    n_pairs = HEADS_PER_GROUP // 2

    @pl.when(ci == 0)
    def _init():
        tail_scr[...] = cbuf_ref[0]
        for g in range(SSM_GROUPS):
            for j in range(n_pairs):
                h2 = h0_ref[0, pl.ds(g * HEADS_PER_GROUP + 2 * j, 2)]
                state_scr[g, :, j * LANES:(j + 1) * LANES] = h2.reshape(2 * SSM_HEAD_DIM, D_STATE).T

    xr = zx_ref[0, :, D_INNER:ZX_DIM]
    xcat = jnp.concatenate([tail_scr[...], xr], axis=0)
    conv = convb_ref[...] + convw_ref[CONV_W - 1:CONV_W, :] * xr
    for k in range(1, CONV_W):
        conv = conv + convw_ref[CONV_W - 1 - k:CONV_W - k, :] * xcat[SUBLANES - k:SUBLANES - k + tc]
    tail_scr[...] = xr[tc - SUBLANES:tc]
    xbc = _silu(conv)
    xs = xbc[:, :D_INNER]
    bm = xbc[:, D_INNER:D_INNER + SSM_GROUPS * D_STATE]
    cm = xbc[:, D_INNER + SSM_GROUPS * D_STATE:]

    dtl = dt_ref[0] + dtb_ref[...]
    dt = jnp.maximum(dtl, 0.0) + jnp.log1p(jnp.exp(-jnp.abs(dtl)))
    row = lax.broadcasted_iota(jnp.int32, (tc, LANES), 0)
    if valid is not None:
        dt = jnp.where(row < valid, dt, 0.0)
    la = dt * (-jnp.exp(alog_ref[...]))
    cs3 = jnp.dot(tri_ref[...], _split3(la), preferred_element_type=F32)
    cs = cs3[:, :LANES] + cs3[:, LANES:2 * LANES] + cs3[:, 2 * LANES:]
    cs_last = cs[tc - 1:tc, :]
    ecs = jnp.exp(cs)
    to_end = jnp.exp(cs_last - cs)
    cdecay = jnp.exp(jnp.broadcast_to(cs_last, (SUBLANES, LANES)))

    def expand(v, e):
        return jnp.dot(_split3(v), e, preferred_element_type=F32)

    e_mat = e_ref[...]
    dt_full = expand(dt, e_mat)
    ecs_full = expand(ecs, e_mat)
    toend_full = expand(to_end, e_mat)
    cdecay_full = expand(cdecay, e_mat)[0:1, :]
    cs_col = expand(cs, e2_ref[...])
    cs_row = cs.T

    xdt = xs * dt_full
    xdt_b = xdt.astype(BF16)
    xw_b = (xdt * toend_full).astype(BF16)
    z = zx_ref[0, :, :D_INNER]

    tri_mask = lax.broadcasted_iota(jnp.int32, (tc, tc), 0) >= lax.broadcasted_iota(jnp.int32, (tc, tc), 1)
    lane_lo = lax.broadcasted_iota(jnp.int32, (tc, LANES), 1) < SSM_HEAD_DIM

    for g in range(SSM_GROUPS):
        gs = slice(g * gw, (g + 1) * gw)
        b_g = bm[:, g * D_STATE:(g + 1) * D_STATE]
        c_g = cm[:, g * D_STATE:(g + 1) * D_STATE].astype(BF16)
        cb = lax.dot_general(c_g, b_g.astype(BF16), (((1,), (1,)), ((), ())), preferred_element_type=F32)
        bt_g = b_g.T.astype(BF16)
        st_old = state_scr[g]
        y_off = jnp.dot(c_g, st_old.astype(BF16), preferred_element_type=F32)
        s_new = jnp.dot(bt_g, xw_b[:, gs], preferred_element_type=F32)
        state_scr[g] = st_old * cdecay_full[:, gs] + s_new
        pairs = []
        for j in range(n_pairs):
            x_pair = xdt_b[:, g * gw + j * LANES:g * gw + (j + 1) * LANES]
            acc = None
            for r2 in range(2):
                h = g * HEADS_PER_GROUP + 2 * j + r2
                seg = cs_col[:, h * LANES:(h + 1) * LANES] - cs_row[h:h + 1, :]
                m = (jnp.exp(jnp.where(tri_mask, seg, NEG_BIG)) * cb).astype(BF16)
                keep = lane_lo if r2 == 0 else jnp.logical_not(lane_lo)
                rhs = jnp.where(keep, x_pair, jnp.zeros_like(x_pair))
                part = jnp.dot(m, rhs, preferred_element_type=F32)
                acc = part if acc is None else acc + part
            pairs.append(acc)
        y = jnp.concatenate(pairs, axis=1) + y_off * ecs_full[:, gs] + xs[:, gs] * dfull_ref[:, gs]
        gated = y * _silu(z[:, gs])
        ms = jnp.mean(gated * gated, axis=-1, keepdims=True)
        g_ref[0, :, gs] = (gated * lax.rsqrt(ms + GATED_NORM_EPS) * gnorm_ref[:, gs]).astype(g_ref.dtype)

    @pl.when(ci == nc - 1)
    def _fin():
        for g in range(SSM_GROUPS):
            for j in range(n_pairs):
                st = state_scr[g, :, j * LANES:(j + 1) * LANES].T
                ht_ref[0, pl.ds(g * HEADS_PER_GROUP + 2 * j, 2)] = st.reshape(2, SSM_HEAD_DIM, D_STATE)


def _ssd_constants(tc):
    tri = (jnp.arange(tc)[:, None] >= jnp.arange(tc)[None, :]).astype(BF16)
    head = jnp.arange(LANES)

    def expansion(width):
        cols = jnp.arange(SSM_HEADS * width) // width
        one = ((head[:, None] == cols[None, :]) & (head[:, None] < SSM_HEADS)).astype(BF16)
        return jnp.concatenate([one, one, one], axis=0)

    return tri, expansion(SSM_HEAD_DIM), expansion(LANES)


def _ssd(zx, dtr, cbuf8, h0, conv_w, conv_b, dt_bias, a_log, d_skip, gate_norm, valid):
    b, lp, _ = zx.shape
    tc = SSD_CHUNK
    nc = lp // tc
    tri, e_mat, e2_mat = _ssd_constants(tc)
    pad = LANES - SSM_HEADS
    dtb = jnp.pad(dt_bias, (0, pad)).reshape(1, LANES)
    alog = jnp.pad(a_log, (0, pad)).reshape(1, LANES)
    dfull = jnp.repeat(d_skip, SSM_HEAD_DIM).reshape(1, D_INNER)
    const = lambda *shape: pl.BlockSpec(shape, lambda bi, ci: (0,) * len(shape))
    return pl.pallas_call(
        functools.partial(_ssd_kernel, tc=tc, valid=valid),
        grid=(b, nc),
        in_specs=[pl.BlockSpec((1, tc, ZX_DIM), lambda bi, ci: (bi, ci, 0)),
                  pl.BlockSpec((1, tc, LANES), lambda bi, ci: (bi, ci, 0)),
                  pl.BlockSpec((1, SUBLANES, CONV_DIM), lambda bi, ci: (bi, 0, 0)),
                  pl.BlockSpec((1, SSM_HEADS, SSM_HEAD_DIM, D_STATE), lambda bi, ci: (bi, 0, 0, 0)),
                  const(CONV_W, CONV_DIM), const(1, CONV_DIM), const(1, LANES), const(1, LANES),
                  const(1, D_INNER), const(1, D_INNER),
                  const(tc, tc), const(3 * LANES, D_INNER), const(3 * LANES, SSM_HEADS * LANES)],
        out_specs=[pl.BlockSpec((1, tc, D_INNER), lambda bi, ci: (bi, ci, 0)),
                   pl.BlockSpec((1, SSM_HEADS, SSM_HEAD_DIM, D_STATE), lambda bi, ci: (bi, 0, 0, 0))],
        out_shape=[jax.ShapeDtypeStruct((b, lp, D_INNER), BF16),
                   jax.ShapeDtypeStruct((b, SSM_HEADS, SSM_HEAD_DIM, D_STATE), F32)],
        scratch_shapes=[pltpu.VMEM((SSM_GROUPS, D_STATE, HEADS_PER_GROUP * SSM_HEAD_DIM), F32),
                        pltpu.VMEM((SUBLANES, CONV_DIM), F32)],
        compiler_params=_params("parallel", "arbitrary"),
        name="ssd",
    )(zx, dtr, cbuf8, h0, conv_w, conv_b.reshape(1, CONV_DIM), dtb, alog, dfull,
      gate_norm.reshape(1, D_INNER), tri, e_mat, e2_mat)


def _mamba_layer(x, conv_buf, ssm_state, p, j):
    b, l, d = x.shape
    t = b * l
    w_in = p['ssm_w_in'][j]
    w_zx = w_in[:, :ZX_DIM].astype(BF16)
    w_dt = jnp.pad(w_in[:, ZX_DIM:], ((0, 0), (0, LANES - SSM_HEADS))).astype(BF16)
    zx, dtr = _norm_matmul(x.reshape(t, d), p['ssm_norm'][j], [w_zx], w_small=w_dt)
    zx = zx.reshape(b, l, ZX_DIM)
    dtr = dtr.reshape(b, l, LANES)
    xbc_raw = zx[:, :, D_INNER:]
    new_conv = jnp.concatenate([conv_buf, xbc_raw[:, max(l - (CONV_W - 1), 0):]], axis=1)[:, -(CONV_W - 1):]
    lp = -(-l // SSD_CHUNK) * SSD_CHUNK
    valid = None
    if lp != l:
        valid = l
        zx = jnp.pad(zx, ((0, 0), (0, lp - l), (0, 0)))
        dtr = jnp.pad(dtr, ((0, 0), (0, lp - l), (0, 0)))
    cbuf8 = jnp.pad(conv_buf, ((0, 0), (SUBLANES - (CONV_W - 1), 0), (0, 0)))
    gated, new_state = _ssd(zx, dtr, cbuf8, ssm_state, p['ssm_conv_w'][j], p['ssm_conv_b'][j],
                            p['ssm_dt_bias'][j], p['ssm_a_log'][j], p['ssm_d'][j], p['ssm_gate_norm'][j], valid)
    gated = gated[:, :l].reshape(t, D_INNER)
    x_new = _matmul_res(gated, p['ssm_w_out'][j].astype(BF16), x.reshape(t, d))
    return x_new.reshape(b, l, d), new_conv, new_state


def _router_kernel(x_ref, g_ref, wr_ref, h_ref, route_ref):
    x = x_ref[...]
    ms = jnp.mean(x * x, axis=-1, keepdims=True)
    h = x * lax.rsqrt(ms + RMS_EPS) * g_ref[...]
    h1 = h.astype(BF16)
    h_ref[...] = h1
    h2 = (h - h1.astype(F32)).astype(BF16)
    logits = jnp.dot(jnp.concatenate([h1, h1, h2], axis=1), wr_ref[...], preferred_element_type=F32)
    tm = x.shape[0]
    lane = lax.broadcasted_iota(jnp.int32, (tm, LANES), 1)
    gl = jnp.where(lane < N_EGROUPS, logits, NEG_BIG)
    gmax = jnp.max(gl, axis=-1, keepdims=True)
    g_sel = jnp.min(jnp.where(gl == gmax, lane, LANES), axis=-1, keepdims=True)
    p_group = 1.0 / jnp.sum(jnp.exp(gl - gmax), axis=-1, keepdims=True)
    lo = N_EGROUPS + EXPERTS_PER_GROUP * g_sel
    el = jnp.where((lane >= lo) & (lane < lo + EXPERTS_PER_GROUP), logits, NEG_BIG)
    v1 = jnp.max(el, axis=-1, keepdims=True)
    i1 = jnp.min(jnp.where(el == v1, lane, LANES), axis=-1, keepdims=True)
    el2 = jnp.where(lane == i1, NEG_BIG, el)
    v2 = jnp.max(el2, axis=-1, keepdims=True)
    i2 = jnp.min(jnp.where(el2 == v2, lane, LANES), axis=-1, keepdims=True)
    e2 = jnp.exp(v2 - v1)
    gate1 = p_group / (1.0 + e2)
    gate2 = p_group * e2 / (1.0 + e2)
    route_ref[...] = jnp.where(lane == 0, (i1 - N_EGROUPS).astype(F32),
                               jnp.where(lane == 1, (i2 - N_EGROUPS).astype(F32),
                                         jnp.where(lane == 2, gate1, jnp.where(lane == 3, gate2, 0.0))))


def _router(x, gain, w_group, w_expert, tm=512):
    t, d = x.shape
    tm = min(tm, t)
    wr = jnp.concatenate([w_group, w_expert], axis=1)
    wr = jnp.pad(wr, ((0, 0), (0, LANES - wr.shape[1])))
    w1 = wr.astype(BF16)
    w2 = (wr - w1.astype(F32)).astype(BF16)
    wr3 = jnp.concatenate([w1, w2, w1], axis=0)
    return pl.pallas_call(
        _router_kernel,
        grid=(t // tm,),
        in_specs=[pl.BlockSpec((tm, d), lambda i: (i, 0)),
                  pl.BlockSpec((1, d), lambda i: (0, 0)),
                  pl.BlockSpec((3 * d, LANES), lambda i: (0, 0))],
        out_specs=[pl.BlockSpec((tm, d), lambda i: (i, 0)),
                   pl.BlockSpec((tm, LANES), lambda i: (i, 0))],
        out_shape=[jax.ShapeDtypeStruct((t, d), BF16), jax.ShapeDtypeStruct((t, LANES), F32)],
        compiler_params=_params("parallel"),
        name="moe_router",
    )(x, gain.reshape(1, d), wr3)


def _grouped_mlp_kernel(be_ref, nu_ref, xs_ref, wg_ref, wu_ref, wd_ref, o_ref, wg_s, wu_s, wd_s):
    b = pl.program_id(0)
    changed = jnp.logical_or(b == 0, be_ref[b] != be_ref[jnp.maximum(b - 1, 0)])

    @pl.when(changed)
    def _():
        wg_s[...] = wg_ref[0].astype(BF16)
        wu_s[...] = wu_ref[0].astype(BF16)
        wd_s[...] = wd_ref[0].astype(BF16)

    @pl.when(b < nu_ref[0])
    def _():
        x = xs_ref[...]
        gate = jnp.dot(x, wg_s[...], preferred_element_type=F32)
        up = jnp.dot(x, wu_s[...], preferred_element_type=F32)
        hid = (_silu(gate) * up).astype(BF16)
        o_ref[...] = jnp.dot(hid, wd_s[...], preferred_element_type=F32)

    @pl.when(b >= nu_ref[0])
    def _():
        o_ref[...] = jnp.zeros_like(o_ref)


def _grouped_mlp(xs, block_expert, n_used, w_gate, w_up, w_down):
    n_slots, d = xs.shape
    ff = w_gate.shape[2]
    n_blocks = n_slots // MOE_ROWS
    grid_spec = pltpu.PrefetchScalarGridSpec(
        num_scalar_prefetch=2,
        grid=(n_blocks,),
        in_specs=[pl.BlockSpec((MOE_ROWS, d), lambda b, be, nu: (b, 0)),
                  pl.BlockSpec((1, d, ff), lambda b, be, nu: (be[b], 0, 0)),
                  pl.BlockSpec((1, d, ff), lambda b, be, nu: (be[b], 0, 0)),
                  pl.BlockSpec((1, ff, d), lambda b, be, nu: (be[b], 0, 0))],
        out_specs=pl.BlockSpec((MOE_ROWS, d), lambda b, be, nu: (b, 0)),
        scratch_shapes=[pltpu.VMEM((d, ff), BF16), pltpu.VMEM((d, ff), BF16), pltpu.VMEM((ff, d), BF16)],
    )
    return pl.pallas_call(
        _grouped_mlp_kernel,
        grid_spec=grid_spec,
        out_shape=jax.ShapeDtypeStruct((n_slots, d), F32),
        compiler_params=_params("arbitrary"),
        name="moe_experts",
    )(block_expert, n_used, xs, w_gate, w_up, w_down)


def _combine_kernel(x_ref, y0_ref, y1_ref, route_ref, gain_ref, o_ref, *, final):
    r = route_ref[...]
    xn = x_ref[...] + r[:, 2:3] * y0_ref[...] + r[:, 3:4] * y1_ref[...]
    if final:
        ms = jnp.mean(xn * xn, axis=-1, keepdims=True)
        xn = xn * lax.rsqrt(ms + RMS_EPS) * gain_ref[...]
    o_ref[...] = xn


def _combine(x, y0, y1, route, final_gain, tm=512):
    t, d = x.shape
    tm = min(tm, t)
    final = final_gain is not None
    gain = (final_gain if final else jnp.ones((d,), F32)).reshape(1, d)
    row = pl.BlockSpec((tm, d), lambda i: (i, 0))
    return pl.pallas_call(
        functools.partial(_combine_kernel, final=final),
        grid=(t // tm,),
        in_specs=[row, row, row, pl.BlockSpec((tm, LANES), lambda i: (i, 0)),
                  pl.BlockSpec((1, d), lambda i: (0, 0))],
        out_specs=row,
        out_shape=jax.ShapeDtypeStruct((t, d), F32),
        compiler_params=_params("parallel"),
        name="moe_combine",
    )(x, y0, y1, route, gain)


def _moe_layer(x, p, i, final_gain=None):
    t, d = x.shape
    hb, route = _router(x, p['moe_norm'][i], p['moe_w_group'][i], p['moe_w_expert'][i])
    flat_e = route[:, :TOP_K].astype(jnp.int32).reshape(t * TOP_K)
    a = t * TOP_K
    r = MOE_ROWS
    order = jnp.argsort(flat_e, stable=True)
    se = flat_e[order]
    counts = jnp.sum(flat_e[:, None] == jnp.arange(N_EXPERTS, dtype=jnp.int32)[None, :], axis=0, dtype=jnp.int32)
    padded = (counts + r - 1) // r * r
    pad_end = jnp.cumsum(padded)
    pad_start = pad_end - padded
    start = jnp.cumsum(counts) - counts
    dest_sorted = pad_start[se] + jnp.arange(a, dtype=jnp.int32) - start[se]
    n_blocks = (a + N_EXPERTS * (r - 1)) // r
    n_slots = n_blocks * r
    slot_tok = jnp.zeros((n_slots,), jnp.int32).at[dest_sorted].set((order // TOP_K).astype(jnp.int32))
    dest = jnp.zeros((a,), jnp.int32).at[order].set(dest_sorted.astype(jnp.int32))
    n_used = (pad_end[-1] // r).astype(jnp.int32)
    blk = jnp.arange(n_blocks, dtype=jnp.int32)
    block_expert = jnp.minimum(jnp.searchsorted(pad_end, blk * r, side='right'), N_EXPERTS - 1).astype(jnp.int32)
    last_e = block_expert[jnp.maximum(n_used - 1, 0)]
    block_expert = jnp.where(blk < n_used, block_expert, last_e)
    xs = hb[slot_tok]
    yb = _grouped_mlp(xs, block_expert, n_used.reshape(1), p['moe_w_gate'][i], p['moe_w_up'][i],
                      p['moe_w_down'][i])
    dest2 = dest.reshape(t, TOP_K)
    return _combine(x, yb[dest2[:, 0]], yb[dest2[:, 1]], route, final_gain)


def _attn_finish(o_halves, tq, lam, subln, out_scale):
    od = o_halves[:tq] - lam * o_halves[tq:]
    ms = jnp.mean(od * od, axis=-1, keepdims=True)
    return od * lax.rsqrt(ms + RMS_EPS) * subln * out_scale


def _attn_kernel(lam_ref, sub_ref, q_ref, k_ref, v_ref, o_ref, kb_scr, vb_scr, *, tq, out_scale):
    qi = pl.program_id(2)

    @pl.when(qi == 0)
    def _():
        kb_scr[...] = k_ref[0].astype(BF16)
        vb_scr[...] = v_ref[0].astype(BF16)

    q = q_ref[0] * ATTN_SCALE
    lane = lax.broadcasted_iota(jnp.int32, (tq, HEAD_W), 1)
    qs = jnp.concatenate([jnp.where(lane < HALF_DIM, q, 0.0), jnp.where(lane >= HALF_DIM, q, 0.0)],
                         axis=0).astype(BF16)

    def step(j, carry, masked):
        m, l, acc = carry
        off = pl.multiple_of(j * tq, tq)
        kblk = kb_scr[pl.ds(off, tq), :]
        vblk = vb_scr[pl.ds(off, tq), :]
        s = lax.dot_general(qs, kblk, (((1,), (1,)), ((), ())), preferred_element_type=F32)
        if masked:
            rq = lax.broadcasted_iota(jnp.int32, (2 * tq, tq), 0)
            rq = jnp.where(rq >= tq, rq - tq, rq)
            ck = lax.broadcasted_iota(jnp.int32, (2 * tq, tq), 1)
            s = jnp.where(ck <= rq, s, NEG_BIG)
        m_new = jnp.maximum(m, jnp.max(s, axis=-1, keepdims=True))
        pexp = jnp.exp(s - m_new)
        corr = jnp.exp(m - m_new)
        l = l * corr + jnp.sum(pexp, axis=-1, keepdims=True)
        acc = acc * corr + jnp.dot(pexp.astype(BF16), vblk, preferred_element_type=F32)
        return m_new, l, acc

    init = (jnp.full((2 * tq, 1), NEG_BIG, F32), jnp.zeros((2 * tq, 1), F32), jnp.zeros((2 * tq, HEAD_W), F32))
    carry = lax.fori_loop(0, qi, lambda j, c: step(j, c, False), init)
    m, l, acc = step(qi, carry, True)
    o_ref[0] = _attn_finish(acc / l, tq, lam_ref[...], sub_ref[...], out_scale).astype(o_ref.dtype)


def _prompt_attention(q, k, v, lam, subln, out_scale, b, s):
    d = q.shape[1]
    tq = min(ATTN_TILE, s)
    q3, k3, v3 = (a.reshape(b, s, d) for a in (q, k, v))
    lam_row = jnp.full((1, HEAD_W), lam, F32)
    out = pl.pallas_call(
        functools.partial(_attn_kernel, tq=tq, out_scale=out_scale),
        grid=(b, ATTN_HEADS, s // tq),
        in_specs=[pl.BlockSpec((1, HEAD_W), lambda bi, h, qi: (0, 0)),
                  pl.BlockSpec((1, HEAD_W), lambda bi, h, qi: (0, 0)),
                  pl.BlockSpec((1, tq, HEAD_W), lambda bi, h, qi: (bi, qi, h)),
                  pl.BlockSpec((1, s, HEAD_W), lambda bi, h, qi: (bi, 0, h)),
                  pl.BlockSpec((1, s, HEAD_W), lambda bi, h, qi: (bi, 0, h))],
        out_specs=pl.BlockSpec((1, tq, HEAD_W), lambda bi, h, qi: (bi, qi, h)),
        out_shape=jax.ShapeDtypeStruct((b, s, d), BF16),
        scratch_shapes=[pltpu.VMEM((s, HEAD_W), BF16), pltpu.VMEM((s, HEAD_W), BF16)],
        compiler_params=_params("parallel", "parallel", "arbitrary"),
        name="prompt_attention",
    )(lam_row, subln.reshape(1, HEAD_W), q3, k3, v3)
    return out.reshape(b * s, d)


def _paged_kernel(pt_ref, lam_ref, sub_ref, qm_ref, kn_ref, vn_ref, *rest, n_pg, tqn, out_scale):
    k_refs = rest[:n_pg]
    v_refs = rest[n_pg:2 * n_pg]
    o_ref = rest[2 * n_pg]
    m_scr, l_scr, acc_scr = rest[2 * n_pg + 1:]
    si = pl.program_id(1)
    ns = pl.num_programs(1)
    rows = ATTN_HEADS * 2 * tqn
    cols = PAGE_SIZE * ATTN_HEADS

    @pl.when(si == 0)
    def _():
        m_scr[...] = jnp.full_like(m_scr, NEG_BIG)
        l_scr[...] = jnp.zeros_like(l_scr)
        acc_scr[...] = jnp.zeros_like(acc_scr)

    qm = qm_ref[0].astype(BF16)
    nt = (((1,), (1,)), ((), ()))
    same_head = (lax.broadcasted_iota(jnp.int32, (rows, cols), 0) // (2 * tqn)
                 == lax.broadcasted_iota(jnp.int32, (rows, cols), 1) % ATTN_HEADS)
    scores = []
    for g in range(n_pg):
        k2 = k_refs[g][0].reshape(cols, HEAD_W).astype(BF16)
        s = lax.dot_general(qm, k2, nt, preferred_element_type=F32)
        scores.append(jnp.where(same_head, s, NEG_BIG))
    m_old = m_scr[...]
    m_new = m_old
    for s in scores:
        m_new = jnp.maximum(m_new, jnp.max(s, axis=-1, keepdims=True))
    corr = jnp.exp(m_old - m_new)
    l = l_scr[...] * corr
    acc = acc_scr[...] * corr
    for g in range(n_pg):
        pexp = jnp.exp(scores[g] - m_new)
        l = l + jnp.sum(pexp, axis=-1, keepdims=True)
        v2 = v_refs[g][0].reshape(cols, HEAD_W).astype(BF16)
        acc = acc + jnp.dot(pexp.astype(BF16), v2, preferred_element_type=F32)
    m_scr[...] = m_new
    l_scr[...] = l
    acc_scr[...] = acc

    @pl.when(si == ns - 1)
    def _():
        npad = kn_ref.shape[1]
        s = lax.dot_general(qm, kn_ref[0].astype(BF16), nt, preferred_element_type=F32)
        r = lax.broadcasted_iota(jnp.int32, (rows, npad), 0)
        c = lax.broadcasted_iota(jnp.int32, (rows, npad), 1)
        ok = (c < tqn * ATTN_HEADS) & (c % ATTN_HEADS == r // (2 * tqn)) & (c // ATTN_HEADS <= r % tqn)
        s = jnp.where(ok, s, NEG_BIG)
        m1 = jnp.maximum(m_new, jnp.max(s, axis=-1, keepdims=True))
        pexp = jnp.exp(s - m1)
        corr1 = jnp.exp(m_new - m1)
        l1 = l * corr1 + jnp.sum(pexp, axis=-1, keepdims=True)
        acc1 = acc * corr1 + jnp.dot(pexp.astype(BF16), vn_ref[0].astype(BF16), preferred_element_type=F32)
        o = acc1 / l1
        od = o - lam_ref[...] * pltpu.roll(o, rows - tqn, 0)
        ms = jnp.mean(od * od, axis=-1, keepdims=True)
        o_ref[0] = od * lax.rsqrt(ms + RMS_EPS) * sub_ref[...] * out_scale


def _paged_attention(q, k_new, v_new, cache_k, cache_v, page_table, lam, subln, out_scale):
    b, n_pages = page_table.shape
    tqn = q.shape[0] // b
    n_pg = PAGES_PER_STEP
    rows = ATTN_HEADS * 2 * tqn
    q4 = q.reshape(b, tqn, ATTN_HEADS, HEAD_W).transpose(0, 2, 1, 3) * ATTN_SCALE
    lane = jnp.arange(HEAD_W)
    qm = jnp.stack([jnp.where(lane < HALF_DIM, q4, 0.0), jnp.where(lane >= HALF_DIM, q4, 0.0)], axis=2)
    qm = qm.reshape(b, rows, HEAD_W)
    npad = LANES
    kn = jnp.pad(k_new.reshape(b, tqn * ATTN_HEADS, HEAD_W), ((0, 0), (0, npad - tqn * ATTN_HEADS), (0, 0)))
    vn = jnp.pad(v_new.reshape(b, tqn * ATTN_HEADS, HEAD_W), ((0, 0), (0, npad - tqn * ATTN_HEADS), (0, 0)))
    lam_row = jnp.full((1, HEAD_W), lam, F32)

    def page_spec(g):
        return pl.BlockSpec((1, PAGE_SIZE, ATTN_HEADS, HEAD_W),
                            lambda bi, si, pt: (pt[bi, si * n_pg + g], 0, 0, 0))

    per_seq = lambda r: pl.BlockSpec((1, r, HEAD_W), lambda bi, si, pt: (bi, 0, 0))
    grid_spec = pltpu.PrefetchScalarGridSpec(
        num_scalar_prefetch=1,
        grid=(b, n_pages // n_pg),
        in_specs=[pl.BlockSpec((1, HEAD_W), lambda bi, si, pt: (0, 0)),
                  pl.BlockSpec((1, HEAD_W), lambda bi, si, pt: (0, 0)),
                  per_seq(rows), per_seq(npad), per_seq(npad)]
                 + [page_spec(g) for g in range(n_pg)] + [page_spec(g) for g in range(n_pg)],
        out_specs=per_seq(rows),
        scratch_shapes=[pltpu.VMEM((rows, 1), F32), pltpu.VMEM((rows, 1), F32), pltpu.VMEM((rows, HEAD_W), F32)],
    )
    o = pl.pallas_call(
        functools.partial(_paged_kernel, n_pg=n_pg, tqn=tqn, out_scale=out_scale),
        grid_spec=grid_spec,
        out_shape=jax.ShapeDtypeStruct((b, rows, HEAD_W), F32),
        compiler_params=_params("parallel", "arbitrary"),
        name="paged_attention",
    )(page_table, lam_row, subln.reshape(1, HEAD_W), qm, kn, vn, *([cache_k] * n_pg), *([cache_v] * n_pg))
    o = o.reshape(b, ATTN_HEADS, 2, tqn, HEAD_W)[:, :, 0]
    return o.transpose(0, 2, 1, 3).reshape(b * tqn, ATTN_HEADS * HEAD_W).astype(BF16)


def _trunk(x, conv_buf, ssm_state, p, attention):
    b, l, d = x.shape
    t = b * l
    x, new_conv, new_state = _mamba_layer(x, conv_buf, ssm_state, p, 0)
    x = _moe_layer(x.reshape(t, d), p, 0)
    k, v = _norm_matmul(x, p['kv_norm'], [p['w_k'].astype(BF16), p['w_v'].astype(BF16)])
    layer = N_A_LAYERS
    lam_init = 0.8 - 0.6 * math.exp(-0.3 * layer)
    lq = p['attn_lambda'][0]
    lam = jnp.exp(jnp.sum(lq[0] * lq[1])) - jnp.exp(jnp.sum(lq[2] * lq[3])) + lam_init
    q, = _norm_matmul(x, p['attn_norm'][0], [p['attn_w_q'][0].astype(BF16)])
    o = attention(q, k, v, lam, p['attn_subln'][0], 1.0 - lam_init)
    x = _matmul_res(o, p['attn_w_o'][0].astype(BF16), x)
    y = _moe_layer(x, p, layer, final_gain=p['final_norm'])
    kv_shape = (b, l, ATTN_HEADS, HEAD_W)
    return y.reshape(b, l, d), new_conv[None], new_state[None], k.reshape(kv_shape), v.reshape(kv_shape)


def kernel(x_prompt, x_sample, state_ssm, state_conv, cache_k, cache_v, page_table, ssm_norm, ssm_w_in, ssm_conv_w, ssm_conv_b, ssm_dt_bias, ssm_a_log, ssm_d, ssm_gate_norm, ssm_w_out, kv_norm, w_k, w_v, attn_norm, attn_w_q, attn_lambda, attn_subln, attn_w_o, moe_norm, moe_w_group, moe_w_expert, moe_w_gate, moe_w_up, moe_w_down, final_norm):
    p = dict(ssm_norm=ssm_norm, ssm_w_in=ssm_w_in, ssm_conv_w=ssm_conv_w, ssm_conv_b=ssm_conv_b,
             ssm_dt_bias=ssm_dt_bias, ssm_a_log=ssm_a_log, ssm_d=ssm_d, ssm_gate_norm=ssm_gate_norm,
             ssm_w_out=ssm_w_out, kv_norm=kv_norm, w_k=w_k, w_v=w_v, attn_norm=attn_norm,
             attn_w_q=attn_w_q, attn_lambda=attn_lambda, attn_subln=attn_subln, attn_w_o=attn_w_o,
             moe_norm=moe_norm, moe_w_group=moe_w_group, moe_w_expert=moe_w_expert,
             moe_w_gate=moe_w_gate, moe_w_up=moe_w_up, moe_w_down=moe_w_down, final_norm=final_norm)
    bp, sp, _ = x_prompt.shape
    conv0 = jnp.zeros((bp, CONV_W - 1, CONV_DIM), F32)
    ssm0 = jnp.zeros((bp, SSM_HEADS, SSM_HEAD_DIM, D_STATE), F32)
    prompt_attn = functools.partial(_prompt_attention, b=bp, s=sp)
    y_p, conv_p, ssm_p, k_p, v_p = _trunk(x_prompt, conv0, ssm0, p, prompt_attn)

    def sample_attn(q, k, v, lam, subln, out_scale):
        return _paged_attention(q, k, v, cache_k, cache_v, page_table, lam, subln, out_scale)

    y_s, conv_s, ssm_s, k_s, v_s = _trunk(x_sample, state_conv[0], state_ssm[0], p, sample_attn)
    return (y_p, y_s, ssm_p, conv_p, k_p, v_p, ssm_s, conv_s, k_s, v_s)
```

```python
import functools
import math

import jax
import jax.numpy as jnp
from jax import lax
from jax.experimental import pallas as pl
from jax.experimental.pallas import tpu as pltpu

F32 = jnp.float32
BF16 = jnp.bfloat16

D_MODEL = 1024
DEPTH = 2
N_A_LAYERS = DEPTH // 2
D_INNER = 2 * D_MODEL
SSM_HEAD_DIM = 64
SSM_HEADS = D_INNER // SSM_HEAD_DIM
SSM_GROUPS = 4
HEADS_PER_GROUP = SSM_HEADS // SSM_GROUPS
D_STATE = 128
CONV_W = 4
CONV_DIM = D_INNER + 2 * SSM_GROUPS * D_STATE
ZX_DIM = D_INNER + CONV_DIM
SSD_CHUNK = 128
GATED_NORM_EPS = 1e-5
HALF_DIM = 64
ATTN_HEADS = D_MODEL // (2 * HALF_DIM)
HEAD_W = 2 * HALF_DIM
ATTN_SCALE = HALF_DIM ** -0.5
N_EGROUPS = 4
EXPERTS_PER_GROUP = 8
N_EXPERTS = N_EGROUPS * EXPERTS_PER_GROUP
TOP_K = 2
EXPERT_FF = D_MODEL // 2
RMS_EPS = 1e-6
PAGE_SIZE = 128

LANES = 128
SUBLANES = 8
MOE_ROWS = 256
ATTN_TQ = 256
ATTN_TK = 512
PAGES_PER_STEP = 8
COL_CHUNK = 1024
LOG2E = 1.4426950408889634
NEG_BIG = -1e30
VMEM_LIMIT = 56 * 1024 * 1024


def _params(*sem):
    return pltpu.CompilerParams(dimension_semantics=sem, vmem_limit_bytes=VMEM_LIMIT)


def _silu(x):
    return x * jax.nn.sigmoid(x)


def _split3(x):
    hi = x.astype(BF16)
    r1 = x - hi.astype(F32)
    mid = r1.astype(BF16)
    lo = (r1 - mid.astype(F32)).astype(BF16)
    return jnp.concatenate([hi, mid, lo], axis=1)


def _norm_matmul_kernel(*refs, n_w, has_small):
    x_ref, g_ref = refs[0], refs[1]
    w_refs = refs[2:2 + n_w]
    pos = 2 + n_w
    ws_ref = refs[pos] if has_small else None
    pos += int(has_small)
    o_refs = refs[pos:pos + n_w]
    pos += n_w
    os_ref = refs[pos] if has_small else None

    x = x_ref[...]
    ms = jnp.mean(x * x, axis=-1, keepdims=True)
    h = (x * lax.rsqrt(ms + RMS_EPS) * g_ref[...]).astype(BF16)
    if has_small:
        os_ref[...] = jnp.dot(h, ws_ref[...], preferred_element_type=F32)
    for w_ref, o_ref in zip(w_refs, o_refs):
        n = w_ref.shape[1]
        step = min(COL_CHUNK, n)
        for c in range(0, n, step):
            o_ref[:, c:c + step] = jnp.dot(h, w_ref[:, c:c + step], preferred_element_type=F32)


def _resident(shape):
    return pl.BlockSpec(shape, lambda i: (0,) * len(shape), pipeline_mode=pl.Buffered(1))


def _norm_matmul(x, gain, weights, w_small=None, tm=512):
    t, d = x.shape
    n = weights[0].shape[1]
    tm = min(tm, t)
    n_w = len(weights)
    has_small = w_small is not None
    in_specs = [pl.BlockSpec((tm, d), lambda i: (i, 0)), _resident((1, d))]
    in_specs += [_resident((d, n))] * n_w
    out_shape = [jax.ShapeDtypeStruct((t, n), F32)] * n_w
    out_specs = [pl.BlockSpec((tm, n), lambda i: (i, 0))] * n_w
    args = [x, gain.reshape(1, d)] + list(weights)
    if has_small:
        in_specs.append(_resident((d, LANES)))
        out_shape.append(jax.ShapeDtypeStruct((t, LANES), F32))
        out_specs.append(pl.BlockSpec((tm, LANES), lambda i: (i, 0)))
        args.append(w_small)
    return pl.pallas_call(
        functools.partial(_norm_matmul_kernel, n_w=n_w, has_small=has_small),
        grid=(t // tm,),
        in_specs=in_specs, out_specs=out_specs, out_shape=out_shape,
        compiler_params=_params("parallel"),
        name="norm_matmul",
    )(*args)


def _matmul_res_kernel(a_ref, w_ref, r_ref, o_ref):
    o_ref[...] = r_ref[...] + jnp.dot(a_ref[...], w_ref[...], preferred_element_type=F32)


def _matmul_res(a, w, res, tm=512):
    t, k = a.shape
    n = w.shape[1]
    tm = min(tm, t)
    return pl.pallas_call(
        _matmul_res_kernel,
        grid=(t // tm,),
        in_specs=[pl.BlockSpec((tm, k), lambda i: (i, 0)), _resident((k, n)),
                  pl.BlockSpec((tm, n), lambda i: (i, 0))],
        out_specs=pl.BlockSpec((tm, n), lambda i: (i, 0)),
        out_shape=jax.ShapeDtypeStruct((t, n), F32),
        compiler_params=_params("parallel"),
        name="matmul_res",
    )(a, w, res)


def _ssd_kernel(zx_ref, dt_ref, cbuf_ref, h0_ref, convw_ref, convb_ref, dtb_ref, alog_ref, dfull_ref,
                gnorm_ref, tri_ref, e_ref, e2_ref, g_ref, ht_ref, state_scr, tail_scr, *, tc, valid):
    ci = pl.program_id(1)
    nc = pl.num_programs(1)
    gw = HEADS_PER_GROUP * SSM_HEAD_DIM
    n_pairs = HEADS_PER_GROUP // 2

    @pl.when(ci == 0)
    def _init():
        tail_scr[...] = cbuf_ref[0]
        for g in range(SSM_GROUPS):
            for j in range(n_pairs):
                h2 = h0_ref[0, pl.ds(g * HEADS_PER_GROUP + 2 * j, 2)]
                state_scr[g, :, j * LANES:(j + 1) * LANES] = h2.reshape(2 * SSM_HEAD_DIM, D_STATE).T

    xr = zx_ref[0, :, D_INNER:ZX_DIM]
    xcat = jnp.concatenate([tail_scr[...], xr], axis=0)
    conv = convb_ref[...] + convw_ref[CONV_W - 1:CONV_W, :] * xr
    for k in range(1, CONV_W):
        conv = conv + convw_ref[CONV_W - 1 - k:CONV_W - k, :] * xcat[SUBLANES - k:SUBLANES - k + tc]
    tail_scr[...] = xr[tc - SUBLANES:tc]
    xbc = _silu(conv)
    xs = xbc[:, :D_INNER]
    bm = xbc[:, D_INNER:D_INNER + SSM_GROUPS * D_STATE]
    cm = xbc[:, D_INNER + SSM_GROUPS * D_STATE:]

    dtl = dt_ref[0] + dtb_ref[...]
    dt = jnp.maximum(dtl, 0.0) + jnp.log1p(jnp.exp(-jnp.abs(dtl)))
    row = lax.broadcasted_iota(jnp.int32, (tc, LANES), 0)
    if valid is not None:
        dt = jnp.where(row < valid, dt, 0.0)
    la = dt * (-jnp.exp(alog_ref[...]))
    cs3 = jnp.dot(tri_ref[...], _split3(la), preferred_element_type=F32)
    cs = cs3[:, :LANES] + cs3[:, LANES:2 * LANES] + cs3[:, 2 * LANES:]
    cs_last = cs[tc - 1:tc, :]
    ecs = jnp.exp(cs)
    to_end = jnp.exp(cs_last - cs)
    cdecay = jnp.exp(jnp.broadcast_to(cs_last, (SUBLANES, LANES)))

    def expand(v, e):
        return jnp.dot(_split3(v), e, preferred_element_type=F32)

    e_mat = e_ref[...]
    dt_full = expand(dt, e_mat)
    ecs_full = expand(ecs, e_mat)
    toend_full = expand(to_end, e_mat)
    cdecay_full = expand(cdecay, e_mat)[0:1, :]
    cs_col = expand(cs, e2_ref[...])
    cs_row = cs.T

    xdt = xs * dt_full
    xdt_b = xdt.astype(BF16)
    xw_b = (xdt * toend_full).astype(BF16)
    z = zx_ref[0, :, :D_INNER]

    tri_mask = lax.broadcasted_iota(jnp.int32, (tc, tc), 0) >= lax.broadcasted_iota(jnp.int32, (tc, tc), 1)
    lane_lo = lax.broadcasted_iota(jnp.int32, (tc, LANES), 1) < SSM_HEAD_DIM

    for g in range(SSM_GROUPS):
        gs = slice(g * gw, (g + 1) * gw)
        b_g = bm[:, g * D_STATE:(g + 1) * D_STATE]
        c_g = cm[:, g * D_STATE:(g + 1) * D_STATE].astype(BF16)
        cb = lax.dot_general(c_g, b_g.astype(BF16), (((1,), (1,)), ((), ())), preferred_element_type=F32)
        bt_g = b_g.T.astype(BF16)
        st_old = state_scr[g]
        y_off = jnp.dot(c_g, st_old.astype(BF16), preferred_element_type=F32)
        s_new = jnp.dot(bt_g, xw_b[:, gs], preferred_element_type=F32)
        state_scr[g] = st_old * cdecay_full[:, gs] + s_new
        pairs = []
        for j in range(n_pairs):
            x_pair = xdt_b[:, g * gw + j * LANES:g * gw + (j + 1) * LANES]
            acc = None
            for r2 in range(2):
                h = g * HEADS_PER_GROUP + 2 * j + r2
                seg = cs_col[:, h * LANES:(h + 1) * LANES] - cs_row[h:h + 1, :]
                m = (jnp.exp(jnp.where(tri_mask, seg, NEG_BIG)) * cb).astype(BF16)
                keep = lane_lo if r2 == 0 else jnp.logical_not(lane_lo)
                rhs = jnp.where(keep, x_pair, jnp.zeros_like(x_pair))
                part = jnp.dot(m, rhs, preferred_element_type=F32)
                acc = part if acc is None else acc + part
            pairs.append(acc)
        y = jnp.concatenate(pairs, axis=1) + y_off * ecs_full[:, gs] + xs[:, gs] * dfull_ref[:, gs]
        gated = y * _silu(z[:, gs])
        ms = jnp.mean(gated * gated, axis=-1, keepdims=True)
        g_ref[0, :, gs] = (gated * lax.rsqrt(ms + GATED_NORM_EPS) * gnorm_ref[:, gs]).astype(g_ref.dtype)

    @pl.when(ci == nc - 1)
    def _fin():
        for g in range(SSM_GROUPS):
            for j in range(n_pairs):
                st = state_scr[g, :, j * LANES:(j + 1) * LANES].T
                ht_ref[0, pl.ds(g * HEADS_PER_GROUP + 2 * j, 2)] = st.reshape(2, SSM_HEAD_DIM, D_STATE)


def _ssd_constants(tc):
    tri = (jnp.arange(tc)[:, None] >= jnp.arange(tc)[None, :]).astype(BF16)
    head = jnp.arange(LANES)

    def expansion(width):
        cols = jnp.arange(SSM_HEADS * width) // width
        one = ((head[:, None] == cols[None, :]) & (head[:, None] < SSM_HEADS)).astype(BF16)
        return jnp.concatenate([one, one, one], axis=0)

    return tri, expansion(SSM_HEAD_DIM), expansion(LANES)


def _ssd(zx, dtr, cbuf8, h0, conv_w, conv_b, dt_bias, a_log, d_skip, gate_norm, valid):
    b, lp, _ = zx.shape
    tc = SSD_CHUNK
    nc = lp // tc
    tri, e_mat, e2_mat = _ssd_constants(tc)
    pad = LANES - SSM_HEADS
    dtb = jnp.pad(dt_bias, (0, pad)).reshape(1, LANES)
    alog = jnp.pad(a_log, (0, pad)).reshape(1, LANES)
    dfull = jnp.repeat(d_skip, SSM_HEAD_DIM).reshape(1, D_INNER)
    const = lambda *shape: pl.BlockSpec(shape, lambda bi, ci: (0,) * len(shape))
    return pl.pallas_call(
        functools.partial(_ssd_kernel, tc=tc, valid=valid),
        grid=(b, nc),
        in_specs=[pl.BlockSpec((1, tc, ZX_DIM), lambda bi, ci: (bi, ci, 0)),
                  pl.BlockSpec((1, tc, LANES), lambda bi, ci: (bi, ci, 0)),
                  pl.BlockSpec((1, SUBLANES, CONV_DIM), lambda bi, ci: (bi, 0, 0)),
                  pl.BlockSpec((1, SSM_HEADS, SSM_HEAD_DIM, D_STATE), lambda bi, ci: (bi, 0, 0, 0)),
                  const(CONV_W, CONV_DIM), const(1, CONV_DIM), const(1, LANES), const(1, LANES),
                  const(1, D_INNER), const(1, D_INNER),
                  const(tc, tc), const(3 * LANES, D_INNER), const(3 * LANES, SSM_HEADS * LANES)],
        out_specs=[pl.BlockSpec((1, tc, D_INNER), lambda bi, ci: (bi, ci, 0)),
                   pl.BlockSpec((1, SSM_HEADS, SSM_HEAD_DIM, D_STATE), lambda bi, ci: (bi, 0, 0, 0))],
        out_shape=[jax.ShapeDtypeStruct((b, lp, D_INNER), BF16),
                   jax.ShapeDtypeStruct((b, SSM_HEADS, SSM_HEAD_DIM, D_STATE), F32)],
        scratch_shapes=[pltpu.VMEM((SSM_GROUPS, D_STATE, HEADS_PER_GROUP * SSM_HEAD_DIM), F32),
                        pltpu.VMEM((SUBLANES, CONV_DIM), F32)],
        compiler_params=_params("parallel", "arbitrary"),
        name="ssd",
    )(zx, dtr, cbuf8, h0, conv_w, conv_b.reshape(1, CONV_DIM), dtb, alog, dfull,
      gate_norm.reshape(1, D_INNER), tri, e_mat, e2_mat)


def _mamba_layer(x, conv_buf, ssm_state, p, j):
    b, l, d = x.shape
    t = b * l
    w_in = p['ssm_w_in'][j]
    w_zx = w_in[:, :ZX_DIM].astype(BF16)
    w_dt = jnp.pad(w_in[:, ZX_DIM:], ((0, 0), (0, LANES - SSM_HEADS))).astype(BF16)
    zx, dtr = _norm_matmul(x.reshape(t, d), p['ssm_norm'][j], [w_zx], w_small=w_dt)
    zx = zx.reshape(b, l, ZX_DIM)
    dtr = dtr.reshape(b, l, LANES)
    xbc_raw = zx[:, :, D_INNER:]
    new_conv = jnp.concatenate([conv_buf, xbc_raw[:, max(l - (CONV_W - 1), 0):]], axis=1)[:, -(CONV_W - 1):]
    lp = -(-l // SSD_CHUNK) * SSD_CHUNK
    valid = None
    if lp != l:
        valid = l
        zx = jnp.pad(zx, ((0, 0), (0, lp - l), (0, 0)))
        dtr = jnp.pad(dtr, ((0, 0), (0, lp - l), (0, 0)))
    cbuf8 = jnp.pad(conv_buf, ((0, 0), (SUBLANES - (CONV_W - 1), 0), (0, 0)))
    gated, new_state = _ssd(zx, dtr, cbuf8, ssm_state, p['ssm_conv_w'][j], p['ssm_conv_b'][j],
                            p['ssm_dt_bias'][j], p['ssm_a_log'][j], p['ssm_d'][j], p['ssm_gate_norm'][j], valid)
    gated = gated[:, :l].reshape(t, D_INNER)
    x_new = _matmul_res(gated, p['ssm_w_out'][j].astype(BF16), x.reshape(t, d))
    return x_new.reshape(b, l, d), new_conv, new_state


def _router_kernel(x_ref, g_ref, wr_ref, low_ref, h_ref, route_ref, cnt_ref, carry_scr):
    @pl.when(pl.program_id(0) == 0)
    def _():
        carry_scr[...] = jnp.zeros_like(carry_scr)

    x = x_ref[...]
    ms = jnp.mean(x * x, axis=-1, keepdims=True)
    h = x * lax.rsqrt(ms + RMS_EPS) * g_ref[...]
    h1 = h.astype(BF16)
    h_ref[...] = h1
    h2 = (h - h1.astype(F32)).astype(BF16)
    logits = jnp.dot(jnp.concatenate([h1, h1, h2], axis=1), wr_ref[...], preferred_element_type=F32)
    tm = x.shape[0]
    lane = lax.broadcasted_iota(jnp.int32, (tm, LANES), 1)
    gl = jnp.where(lane < N_EGROUPS, logits, NEG_BIG)
    gmax = jnp.max(gl, axis=-1, keepdims=True)
    g_sel = jnp.min(jnp.where(gl == gmax, lane, LANES), axis=-1, keepdims=True)
    p_group = 1.0 / jnp.sum(jnp.exp(gl - gmax), axis=-1, keepdims=True)
    lo = N_EGROUPS + EXPERTS_PER_GROUP * g_sel
    el = jnp.where((lane >= lo) & (lane < lo + EXPERTS_PER_GROUP), logits, NEG_BIG)
    v1 = jnp.max(el, axis=-1, keepdims=True)
    i1 = jnp.min(jnp.where(el == v1, lane, LANES), axis=-1, keepdims=True)
    el2 = jnp.where(lane == i1, NEG_BIG, el)
    v2 = jnp.max(el2, axis=-1, keepdims=True)
    i2 = jnp.min(jnp.where(el2 == v2, lane, LANES), axis=-1, keepdims=True)
    e2 = jnp.exp(v2 - v1)
    gate1 = p_group / (1.0 + e2)
    gate2 = p_group * e2 / (1.0 + e2)
    hit1 = lane == i1
    hit2 = lane == i2
    sel = jnp.where(hit1 | hit2, 1.0, 0.0)
    before = jnp.dot(low_ref[...], sel.astype(BF16), preferred_element_type=F32) + carry_scr[0:1, :]
    rank1 = jnp.sum(jnp.where(hit1, before, 0.0), axis=-1, keepdims=True)
    rank2 = jnp.sum(jnp.where(hit2, before, 0.0), axis=-1, keepdims=True)
    carry_scr[...] = carry_scr[...] + jnp.sum(sel, axis=0, keepdims=True)
    cnt_ref[...] = carry_scr[...]
    vals = ((i1 - N_EGROUPS).astype(F32), (i2 - N_EGROUPS).astype(F32), gate1, gate2, rank1, rank2)
    out = jnp.zeros((tm, LANES), F32)
    for k, v in enumerate(vals):
        out = jnp.where(lane == k, v, out)
    route_ref[...] = out


def _router(x, gain, w_group, w_expert, layer, tm=512):
    t, d = x.shape
    tm = min(tm, t)
    wr = jnp.concatenate([w_group, w_expert], axis=2)
    wr = jnp.pad(wr, ((0, 0), (0, 0), (0, LANES - wr.shape[2])))
    w1 = wr.astype(BF16)
    w2 = (wr - w1.astype(F32)).astype(BF16)
    wr3 = jnp.concatenate([w1, w2, w1], axis=1)[layer]
    strict_lower = (jnp.arange(tm)[:, None] > jnp.arange(tm)[None, :]).astype(BF16)
    return pl.pallas_call(
        _router_kernel,
        grid=(t // tm,),
        in_specs=[pl.BlockSpec((tm, d), lambda i: (i, 0)), _resident((1, d)), _resident((3 * d, LANES)),
                  _resident((tm, tm))],
        out_specs=[pl.BlockSpec((tm, d), lambda i: (i, 0)),
                   pl.BlockSpec((tm, LANES), lambda i: (i, 0)),
                   pl.BlockSpec((SUBLANES, LANES), lambda i: (0, 0))],
        out_shape=[jax.ShapeDtypeStruct((t, d), BF16), jax.ShapeDtypeStruct((t, LANES), F32),
                   jax.ShapeDtypeStruct((SUBLANES, LANES), F32)],
        scratch_shapes=[pltpu.VMEM((SUBLANES, LANES), F32)],
        compiler_params=_params("arbitrary"),
        name="moe_router",
    )(x, gain.reshape(1, d), wr3, strict_lower)


def _grouped_mlp_kernel(be_ref, nu_ref, xs_ref, wg_ref, wu_ref, wd_ref, o_ref, wg_s, wu_s, wd_s):
    b = pl.program_id(0)
    changed = jnp.logical_or(b == 0, be_ref[b] != be_ref[jnp.maximum(b - 1, 0)])

    @pl.when(changed)
    def _():
        wg_s[...] = wg_ref[0, 0].astype(BF16)
        wu_s[...] = wu_ref[0, 0].astype(BF16)
        wd_s[...] = wd_ref[0, 0].astype(BF16)

    @pl.when(b < nu_ref[0])
    def _():
        x = xs_ref[...]
        gate = jnp.dot(x, wg_s[...], preferred_element_type=F32)
        up = jnp.dot(x, wu_s[...], preferred_element_type=F32)
        hid = (_silu(gate) * up).astype(BF16)
        o_ref[...] = jnp.dot(hid, wd_s[...], preferred_element_type=F32).astype(o_ref.dtype)

    @pl.when(b >= nu_ref[0])
    def _():
        o_ref[...] = jnp.zeros_like(o_ref)


def _grouped_mlp(xs, block_expert, n_used, w_gate, w_up, w_down, layer):
    n_slots, d = xs.shape
    ff = w_gate.shape[3]
    n_blocks = n_slots // MOE_ROWS
    grid_spec = pltpu.PrefetchScalarGridSpec(
        num_scalar_prefetch=2,
        grid=(n_blocks,),
        in_specs=[pl.BlockSpec((MOE_ROWS, d), lambda b, be, nu: (b, 0)),
                  pl.BlockSpec((1, 1, d, ff), lambda b, be, nu: (layer, be[b], 0, 0)),
                  pl.BlockSpec((1, 1, d, ff), lambda b, be, nu: (layer, be[b], 0, 0)),
                  pl.BlockSpec((1, 1, ff, d), lambda b, be, nu: (layer, be[b], 0, 0))],
        out_specs=pl.BlockSpec((MOE_ROWS, d), lambda b, be, nu: (b, 0)),
        scratch_shapes=[pltpu.VMEM((d, ff), BF16), pltpu.VMEM((d, ff), BF16), pltpu.VMEM((ff, d), BF16)],
    )
    return pl.pallas_call(
        _grouped_mlp_kernel,
        grid_spec=grid_spec,
        out_shape=jax.ShapeDtypeStruct((n_slots, d), BF16),
        compiler_params=_params("arbitrary"),
        name="moe_experts",
    )(block_expert, n_used, xs, w_gate, w_up, w_down)


def _combine_kernel(x_ref, y_ref, route_ref, gain_ref, o_ref, *, final):
    r = route_ref[...]
    d = x_ref.shape[1]
    xn = x_ref[...] + r[:, 2:3] * y_ref[:, :d].astype(F32) + r[:, 3:4] * y_ref[:, d:].astype(F32)
    if final:
        ms = jnp.mean(xn * xn, axis=-1, keepdims=True)
        xn = xn * lax.rsqrt(ms + RMS_EPS) * gain_ref[...]
    o_ref[...] = xn


def _combine(x, y01, route, final_gain, tm=512):
    t, d = x.shape
    tm = min(tm, t)
    final = final_gain is not None
    gain = (final_gain if final else jnp.ones((d,), F32)).reshape(1, d)
    row = pl.BlockSpec((tm, d), lambda i: (i, 0))
    return pl.pallas_call(
        functools.partial(_combine_kernel, final=final),
        grid=(t // tm,),
        in_specs=[row, pl.BlockSpec((tm, TOP_K * d), lambda i: (i, 0)), pl.BlockSpec((tm, LANES), lambda i: (i, 0)),
                  _resident((1, d))],
        out_specs=row,
        out_shape=jax.ShapeDtypeStruct((t, d), F32),
        compiler_params=_params("parallel"),
        name="moe_combine",
    )(x, y01, route, gain)


def _moe_layer(x, p, i, final_gain=None):
    t, d = x.shape
    hb, route, cnt = _router(x, p['moe_norm'][i], p['moe_w_group'], p['moe_w_expert'], i)
    a = t * TOP_K
    r = MOE_ROWS
    i32 = jnp.int32
    e_idx = route[:, :TOP_K].astype(i32)
    rank = route[:, 4:4 + TOP_K].astype(i32)
    counts = cnt[0, N_EGROUPS:N_EGROUPS + N_EXPERTS].astype(i32)
    padded = (counts + r - 1) // r * r
    pad_end = jnp.cumsum(padded)
    pad_start = pad_end - padded
    start = jnp.cumsum(counts) - counts
    dest = (pad_start[e_idx] + rank).reshape(a)
    n_blocks = (a + N_EXPERTS * (r - 1)) // r
    n_used = (pad_end[-1] // r).astype(i32)
    blk = jnp.arange(n_blocks, dtype=i32)
    block_expert = jnp.minimum(jnp.searchsorted(pad_end, blk * r, side='right'), N_EXPERTS - 1).astype(i32)
    order = jnp.argsort(e_idx.reshape(a), stable=True).astype(i32)
    off = blk[:, None] * r + jnp.arange(r, dtype=i32)[None, :] - pad_start[block_expert][:, None]
    live = (off < counts[block_expert][:, None]) & (blk < n_used)[:, None]
    src = jnp.clip(start[block_expert][:, None] + off, 0, a - 1)
    slot_tok = jnp.where(live, order[src.reshape(-1)].reshape(n_blocks, r) // TOP_K, 0).reshape(n_blocks * r)
    last_e = block_expert[jnp.maximum(n_used - 1, 0)]
    block_expert = jnp.where(blk < n_used, block_expert, last_e)
    xs = hb[slot_tok]
    yb = _grouped_mlp(xs, block_expert, n_used.reshape(1), p['moe_w_gate'], p['moe_w_up'], p['moe_w_down'], i)
    y01 = yb[dest].reshape(t, TOP_K * d)
    return _combine(x, y01, route, final_gain)


def _attn_finish(o_halves, tq, lam, subln, out_scale):
    od = o_halves[:tq] - lam * o_halves[tq:]
    ms = jnp.mean(od * od, axis=-1, keepdims=True)
    return od * lax.rsqrt(ms + RMS_EPS) * subln * out_scale


def _attn_kernel(lam_ref, sub_ref, q_ref, k_ref, v_ref, o_ref, kb_scr, vb_scr, *, tq, tk, out_scale):
    qi = pl.program_id(2)

    @pl.when(qi == 0)
    def _():
        kb_scr[...] = k_ref[0].astype(BF16)
        vb_scr[:, :HEAD_W] = v_ref[0].astype(BF16)
        vb_scr[:, HEAD_W:] = jnp.ones((vb_scr.shape[0], HEAD_W), BF16)

    q = q_ref[0] * (ATTN_SCALE * LOG2E)
    lane = lax.broadcasted_iota(jnp.int32, (tq, HEAD_W), 1)
    qs = jnp.concatenate([jnp.where(lane < HALF_DIM, q, 0.0), jnp.where(lane >= HALF_DIM, q, 0.0)],
                         axis=0).astype(BF16)

    def step(j, carry, masked):
        m, acc = carry
        off = pl.multiple_of(j * tk, tk)
        kblk = kb_scr[pl.ds(off, tk), :]
        vblk = vb_scr[pl.ds(off, tk), :]
        s = lax.dot_general(qs, kblk, (((1,), (1,)), ((), ())), preferred_element_type=F32)
        if masked:
            rq = lax.broadcasted_iota(jnp.int32, (2 * tq, tk), 0)
            rq = jnp.where(rq >= tq, rq - tq, rq) + qi * tq
            ck = lax.broadcasted_iota(jnp.int32, (2 * tq, tk), 1) + j * tk
            s = jnp.where(ck <= rq, s, NEG_BIG)
        m_new = jnp.maximum(m, jnp.max(s, axis=-1, keepdims=True))
        pexp = jnp.exp2((s - m_new).astype(BF16))
        acc = acc * jnp.exp2(m - m_new) + jnp.dot(pexp, vblk, preferred_element_type=F32)
        return m_new, acc

    init = (jnp.full((2 * tq, 1), NEG_BIG, F32), jnp.zeros((2 * tq, 2 * HEAD_W), F32))
    n_full = (qi * tq) // tk
    carry = lax.fori_loop(0, n_full, lambda j, c: step(j, c, False), init)
    m, acc = step(n_full, carry, True)
    o = acc[:, :HEAD_W] / acc[:, HEAD_W:]
    o_ref[0] = _attn_finish(o, tq, lam_ref[...], sub_ref[...], out_scale).astype(o_ref.dtype)


def _prompt_attention(q, k, v, lam, subln, out_scale, b, s):
    d = q.shape[1]
    tq = min(ATTN_TQ, s)
    tk = min(ATTN_TK, s)
    q3, k3, v3 = (a.reshape(b, s, d) for a in (q, k, v))
    lam_row = jnp.full((1, HEAD_W), lam, F32)
    out = pl.pallas_call(
        functools.partial(_attn_kernel, tq=tq, tk=tk, out_scale=out_scale),
        grid=(b, ATTN_HEADS, s // tq),
        in_specs=[pl.BlockSpec((1, HEAD_W), lambda bi, h, qi: (0, 0)),
                  pl.BlockSpec((1, HEAD_W), lambda bi, h, qi: (0, 0)),
                  pl.BlockSpec((1, tq, HEAD_W), lambda bi, h, qi: (bi, qi, h)),
                  pl.BlockSpec((1, s, HEAD_W), lambda bi, h, qi: (bi, 0, h)),
                  pl.BlockSpec((1, s, HEAD_W), lambda bi, h, qi: (bi, 0, h))],
        out_specs=pl.BlockSpec((1, tq, HEAD_W), lambda bi, h, qi: (bi, qi, h)),
        out_shape=jax.ShapeDtypeStruct((b, s, d), BF16),
        scratch_shapes=[pltpu.VMEM((s, HEAD_W), BF16), pltpu.VMEM((s, 2 * HEAD_W), BF16)],
        compiler_params=_params("parallel", "parallel", "arbitrary"),
        name="prompt_attention",
    )(lam_row, subln.reshape(1, HEAD_W), q3, k3, v3)
    return out.reshape(b * s, d)


def _paged_kernel(pt_ref, lam_ref, sub_ref, qm_ref, kn_ref, vn_ref, *rest, n_pg, tqn, out_scale):
    k_refs = rest[:n_pg]
    v_refs = rest[n_pg:2 * n_pg]
    o_ref = rest[2 * n_pg]
    m_scr, l_scr, acc_scr = rest[2 * n_pg + 1:]
    si = pl.program_id(1)
    ns = pl.num_programs(1)
    rows = ATTN_HEADS * 2 * tqn
    cols = PAGE_SIZE * ATTN_HEADS

    @pl.when(si == 0)
    def _():
        m_scr[...] = jnp.full_like(m_scr, NEG_BIG)
        l_scr[...] = jnp.zeros_like(l_scr)
        acc_scr[...] = jnp.zeros_like(acc_scr)

    qm = qm_ref[0].astype(BF16)
    nt = (((1,), (1,)), ((), ()))
    same_head = (lax.broadcasted_iota(jnp.int32, (rows, cols), 0) // (2 * tqn)
                 == lax.broadcasted_iota(jnp.int32, (rows, cols), 1) % ATTN_HEADS)
    scores = []
    for g in range(n_pg):
        k2 = k_refs[g][0].reshape(cols, HEAD_W).astype(BF16)
        s = lax.dot_general(qm, k2, nt, preferred_element_type=F32)
        scores.append(jnp.where(same_head, s, NEG_BIG))
    m_old = m_scr[...]
    m_new = m_old
    for s in scores:
        m_new = jnp.maximum(m_new, jnp.max(s, axis=-1, keepdims=True))
    corr = jnp.exp(m_old - m_new)
    l = l_scr[...] * corr
    acc = acc_scr[...] * corr
    for g in range(n_pg):
        pexp = jnp.exp(scores[g] - m_new)
        l = l + jnp.sum(pexp, axis=-1, keepdims=True)
        v2 = v_refs[g][0].reshape(cols, HEAD_W).astype(BF16)
        acc = acc + jnp.dot(pexp.astype(BF16), v2, preferred_element_type=F32)
    m_scr[...] = m_new
    l_scr[...] = l
    acc_scr[...] = acc

    @pl.when(si == ns - 1)
    def _():
        npad = kn_ref.shape[1]
        s = lax.dot_general(qm, kn_ref[0].astype(BF16), nt, preferred_element_type=F32)
        r = lax.broadcasted_iota(jnp.int32, (rows, npad), 0)
        c = lax.broadcasted_iota(jnp.int32, (rows, npad), 1)
        ok = (c < tqn * ATTN_HEADS) & (c % ATTN_HEADS == r // (2 * tqn)) & (c // ATTN_HEADS <= r % tqn)
        s = jnp.where(ok, s, NEG_BIG)
        m1 = jnp.maximum(m_new, jnp.max(s, axis=-1, keepdims=True))
        pexp = jnp.exp(s - m1)
        corr1 = jnp.exp(m_new - m1)
        l1 = l * corr1 + jnp.sum(pexp, axis=-1, keepdims=True)
        acc1 = acc * corr1 + jnp.dot(pexp.astype(BF16), vn_ref[0].astype(BF16), preferred_element_type=F32)
        o = acc1 / l1
        od = o - lam_ref[...] * pltpu.roll(o, rows - tqn, 0)
        ms = jnp.mean(od * od, axis=-1, keepdims=True)
        o_ref[0] = od * lax.rsqrt(ms + RMS_EPS) * sub_ref[...] * out_scale


def _paged_attention(q, k_new, v_new, cache_k, cache_v, page_table, lam, subln, out_scale):
    b, n_pages = page_table.shape
    tqn = q.shape[0] // b
    n_pg = PAGES_PER_STEP
    rows = ATTN_HEADS * 2 * tqn
    q4 = q.reshape(b, tqn, ATTN_HEADS, HEAD_W).transpose(0, 2, 1, 3) * ATTN_SCALE
    lane = jnp.arange(HEAD_W)
    qm = jnp.stack([jnp.where(lane < HALF_DIM, q4, 0.0), jnp.where(lane >= HALF_DIM, q4, 0.0)], axis=2)
    qm = qm.reshape(b, rows, HEAD_W)
    npad = LANES
    kn = jnp.pad(k_new.reshape(b, tqn * ATTN_HEADS, HEAD_W), ((0, 0), (0, npad - tqn * ATTN_HEADS), (0, 0)))
    vn = jnp.pad(v_new.reshape(b, tqn * ATTN_HEADS, HEAD_W), ((0, 0), (0, npad - tqn * ATTN_HEADS), (0, 0)))
    lam_row = jnp.full((1, HEAD_W), lam, F32)

    def page_spec(g):
        return pl.BlockSpec((1, PAGE_SIZE, ATTN_HEADS, HEAD_W),
                            lambda bi, si, pt: (pt[bi, si * n_pg + g], 0, 0, 0))

    per_seq = lambda r: pl.BlockSpec((1, r, HEAD_W), lambda bi, si, pt: (bi, 0, 0))
    grid_spec = pltpu.PrefetchScalarGridSpec(
        num_scalar_prefetch=1,
        grid=(b, n_pages // n_pg),
        in_specs=[pl.BlockSpec((1, HEAD_W), lambda bi, si, pt: (0, 0)),
                  pl.BlockSpec((1, HEAD_W), lambda bi, si, pt: (0, 0)),
                  per_seq(rows), per_seq(npad), per_seq(npad)]
                 + [page_spec(g) for g in range(n_pg)] + [page_spec(g) for g in range(n_pg)],
        out_specs=per_seq(rows),
        scratch_shapes=[pltpu.VMEM((rows, 1), F32), pltpu.VMEM((rows, 1), F32), pltpu.VMEM((rows, HEAD_W), F32)],
    )
    o = pl.pallas_call(
        functools.partial(_paged_kernel, n_pg=n_pg, tqn=tqn, out_scale=out_scale),
        grid_spec=grid_spec,
        out_shape=jax.ShapeDtypeStruct((b, rows, HEAD_W), F32),
        compiler_params=_params("parallel", "arbitrary"),
        name="paged_attention",
    )(page_table, lam_row, subln.reshape(1, HEAD_W), qm, kn, vn, *([cache_k] * n_pg), *([cache_v] * n_pg))
    o = o.reshape(b, ATTN_HEADS, 2, tqn, HEAD_W)[:, :, 0]
    return o.transpose(0, 2, 1, 3).reshape(b * tqn, ATTN_HEADS * HEAD_W).astype(BF16)


def _trunk(x, conv_buf, ssm_state, p, attention):
    b, l, d = x.shape
    t = b * l
    x, new_conv, new_state = _mamba_layer(x, conv_buf, ssm_state, p, 0)
    x = _moe_layer(x.reshape(t, d), p, 0)
    k, v = _norm_matmul(x, p['kv_norm'], [p['w_k'].astype(BF16), p['w_v'].astype(BF16)])
    layer = N_A_LAYERS
    lam_init = 0.8 - 0.6 * math.exp(-0.3 * layer)
    lq = p['attn_lambda'][0]
    lam = jnp.exp(jnp.sum(lq[0] * lq[1])) - jnp.exp(jnp.sum(lq[2] * lq[3])) + lam_init
    q, = _norm_matmul(x, p['attn_norm'][0], [p['attn_w_q'][0].astype(BF16)])
    o = attention(q, k, v, lam, p['attn_subln'][0], 1.0 - lam_init)
    x = _matmul_res(o, p['attn_w_o'][0].astype(BF16), x)
    y = _moe_layer(x, p, layer, final_gain=p['final_norm'])
    kv_shape = (b, l, ATTN_HEADS, HEAD_W)
    return y.reshape(b, l, d), new_conv[None], new_state[None], k.reshape(kv_shape), v.reshape(kv_shape)


def kernel(x_prompt, x_sample, state_ssm, state_conv, cache_k, cache_v, page_table, ssm_norm, ssm_w_in, ssm_conv_w, ssm_conv_b, ssm_dt_bias, ssm_a_log, ssm_d, ssm_gate_norm, ssm_w_out, kv_norm, w_k, w_v, attn_norm, attn_w_q, attn_lambda, attn_subln, attn_w_o, moe_norm, moe_w_group, moe_w_expert, moe_w_gate, moe_w_up, moe_w_down, final_norm):
    p = dict(ssm_norm=ssm_norm, ssm_w_in=ssm_w_in, ssm_conv_w=ssm_conv_w, ssm_conv_b=ssm_conv_b,
             ssm_dt_bias=ssm_dt_bias, ssm_a_log=ssm_a_log, ssm_d=ssm_d, ssm_gate_norm=ssm_gate_norm,
             ssm_w_out=ssm_w_out, kv_norm=kv_norm, w_k=w_k, w_v=w_v, attn_norm=attn_norm,
             attn_w_q=attn_w_q, attn_lambda=attn_lambda, attn_subln=attn_subln, attn_w_o=attn_w_o,
             moe_norm=moe_norm, moe_w_group=moe_w_group, moe_w_expert=moe_w_expert,
             moe_w_gate=moe_w_gate, moe_w_up=moe_w_up, moe_w_down=moe_w_down, final_norm=final_norm)
    bp, sp, _ = x_prompt.shape
    conv0 = jnp.zeros((bp, CONV_W - 1, CONV_DIM), F32)
    ssm0 = jnp.zeros((bp, SSM_HEADS, SSM_HEAD_DIM, D_STATE), F32)
    prompt_attn = functools.partial(_prompt_attention, b=bp, s=sp)
    y_p, conv_p, ssm_p, k_p, v_p = _trunk(x_prompt, conv0, ssm0, p, prompt_attn)

    def sample_attn(q, k, v, lam, subln, out_scale):
        return _paged_attention(q, k, v, cache_k, cache_v, page_table, lam, subln, out_scale)

    y_s, conv_s, ssm_s, k_s, v_s = _trunk(x_sample, state_conv[0], state_ssm[0], p, sample_attn)
    return (y_p, y_s, ssm_p, conv_p, k_p, v_p, ssm_s, conv_s, k_s, v_s)
```

```python
import functools
import math

import jax
import jax.numpy as jnp
from jax import lax
from jax.experimental import pallas as pl
from jax.experimental.pallas import tpu as pltpu

F32 = jnp.float32
BF16 = jnp.bfloat16

D_MODEL = 1024
DEPTH = 2
N_A_LAYERS = DEPTH // 2
D_INNER = 2 * D_MODEL
SSM_HEAD_DIM = 64
SSM_HEADS = D_INNER // SSM_HEAD_DIM
SSM_GROUPS = 4
HEADS_PER_GROUP = SSM_HEADS // SSM_GROUPS
D_STATE = 128
CONV_W = 4
CONV_DIM = D_INNER + 2 * SSM_GROUPS * D_STATE
ZX_DIM = D_INNER + CONV_DIM
SSD_CHUNK = 128
GATED_NORM_EPS = 1e-5
HALF_DIM = 64
ATTN_HEADS = D_MODEL // (2 * HALF_DIM)
HEAD_W = 2 * HALF_DIM
ATTN_SCALE = HALF_DIM ** -0.5
N_EGROUPS = 4
EXPERTS_PER_GROUP = 8
N_EXPERTS = N_EGROUPS * EXPERTS_PER_GROUP
TOP_K = 2
EXPERT_FF = D_MODEL // 2
RMS_EPS = 1e-6
PAGE_SIZE = 128

LANES = 128
SUBLANES = 8
MOE_ROWS = 256
ATTN_TQ = 256
ATTN_TK = 512
PAGES_PER_STEP = 8
COL_CHUNK = 1024
LOG2E = 1.4426950408889634
NEG_BIG = -1e30
VMEM_LIMIT = 56 * 1024 * 1024


def _params(*sem):
    return pltpu.CompilerParams(dimension_semantics=sem, vmem_limit_bytes=VMEM_LIMIT)


def _silu(x):
    return x * jax.nn.sigmoid(x)


def _split3(x):
    hi = x.astype(BF16)
    r1 = x - hi.astype(F32)
    mid = r1.astype(BF16)
    lo = (r1 - mid.astype(F32)).astype(BF16)
    return jnp.concatenate([hi, mid, lo], axis=1)


def _norm_matmul_kernel(*refs, n_w, has_small):
    x_ref, g_ref = refs[0], refs[1]
    w_refs = refs[2:2 + n_w]
    pos = 2 + n_w
    ws_ref = refs[pos] if has_small else None
    pos += int(has_small)
    o_refs = refs[pos:pos + n_w]
    pos += n_w
    os_ref = refs[pos] if has_small else None

    x = x_ref[...]
    ms = jnp.mean(x * x, axis=-1, keepdims=True)
    h = (x * lax.rsqrt(ms + RMS_EPS) * g_ref[...]).astype(BF16)
    if has_small:
        os_ref[...] = jnp.dot(h, ws_ref[...], preferred_element_type=F32)
    for w_ref, o_ref in zip(w_refs, o_refs):
        n = w_ref.shape[1]
        step = min(COL_CHUNK, n)
        for c in range(0, n, step):
            o_ref[:, c:c + step] = jnp.dot(h, w_ref[:, c:c + step], preferred_element_type=F32)


def _resident(shape):
    return pl.BlockSpec(shape, lambda i: (0,) * len(shape), pipeline_mode=pl.Buffered(1))


def _norm_matmul(x, gain, weights, w_small=None, tm=512):
    t, d = x.shape
    n = weights[0].shape[1]
    tm = min(tm, t)
    n_w = len(weights)
    has_small = w_small is not None
    in_specs = [pl.BlockSpec((tm, d), lambda i: (i, 0)), _resident((1, d))]
    in_specs += [_resident((d, n))] * n_w
    out_shape = [jax.ShapeDtypeStruct((t, n), F32)] * n_w
    out_specs = [pl.BlockSpec((tm, n), lambda i: (i, 0))] * n_w
    args = [x, gain.reshape(1, d)] + list(weights)
    if has_small:
        in_specs.append(_resident((d, LANES)))
        out_shape.append(jax.ShapeDtypeStruct((t, LANES), F32))
        out_specs.append(pl.BlockSpec((tm, LANES), lambda i: (i, 0)))
        args.append(w_small)
    return pl.pallas_call(
        functools.partial(_norm_matmul_kernel, n_w=n_w, has_small=has_small),
        grid=(t // tm,),
        in_specs=in_specs, out_specs=out_specs, out_shape=out_shape,
        compiler_params=_params("parallel"),
        name="norm_matmul",
    )(*args)


def _matmul_res_kernel(a_ref, w_ref, r_ref, o_ref):
    o_ref[...] = r_ref[...] + jnp.dot(a_ref[...], w_ref[...], preferred_element_type=F32)


def _matmul_res(a, w, res, tm=512):
    t, k = a.shape
    n = w.shape[1]
    tm = min(tm, t)
    return pl.pallas_call(
        _matmul_res_kernel,
        grid=(t // tm,),
        in_specs=[pl.BlockSpec((tm, k), lambda i: (i, 0)), _resident((k, n)),
                  pl.BlockSpec((tm, n), lambda i: (i, 0))],
        out_specs=pl.BlockSpec((tm, n), lambda i: (i, 0)),
        out_shape=jax.ShapeDtypeStruct((t, n), F32),
        compiler_params=_params("parallel"),
        name="matmul_res",
    )(a, w, res)


def _ssd_kernel(zx_ref, dt_ref, cbuf_ref, h0_ref, convw_ref, convb_ref, dtb_ref, alog_ref, dfull_ref,
                gnorm_ref, tri_ref, e_ref, e2_ref, g_ref, ht_ref, state_scr, tail_scr, *, tc, valid):
    ci = pl.program_id(1)
    nc = pl.num_programs(1)
    gw = HEADS_PER_GROUP * SSM_HEAD_DIM
    n_pairs = HEADS_PER_GROUP // 2

    @pl.when(ci == 0)
    def _init():
        tail_scr[...] = cbuf_ref[0]
        for g in range(SSM_GROUPS):
            for j in range(n_pairs):
                h2 = h0_ref[0, pl.ds(g * HEADS_PER_GROUP + 2 * j, 2)]
                state_scr[g, :, j * LANES:(j + 1) * LANES] = h2.reshape(2 * SSM_HEAD_DIM, D_STATE).T

    xr = zx_ref[0, :, D_INNER:ZX_DIM]
    xcat = jnp.concatenate([tail_scr[...], xr], axis=0)
    conv = convb_ref[...] + convw_ref[CONV_W - 1:CONV_W, :] * xr
    for k in range(1, CONV_W):
        conv = conv + convw_ref[CONV_W - 1 - k:CONV_W - k, :] * xcat[SUBLANES - k:SUBLANES - k + tc]
    tail_scr[...] = xr[tc - SUBLANES:tc]
    xbc = _silu(conv)
    xs = xbc[:, :D_INNER]
    bm = xbc[:, D_INNER:D_INNER + SSM_GROUPS * D_STATE]
    cm = xbc[:, D_INNER + SSM_GROUPS * D_STATE:]

    dtl = dt_ref[0] + dtb_ref[...]
    dt = jnp.maximum(dtl, 0.0) + jnp.log1p(jnp.exp(-jnp.abs(dtl)))
    row = lax.broadcasted_iota(jnp.int32, (tc, LANES), 0)
    if valid is not None:
        dt = jnp.where(row < valid, dt, 0.0)
    la = dt * (-jnp.exp(alog_ref[...]))
    cs3 = jnp.dot(tri_ref[...], _split3(la), preferred_element_type=F32)
    cs = cs3[:, :LANES] + cs3[:, LANES:2 * LANES] + cs3[:, 2 * LANES:]
    cs_last = cs[tc - 1:tc, :]
    ecs = jnp.exp(cs)
    to_end = jnp.exp(cs_last - cs)
    cdecay = jnp.exp(jnp.broadcast_to(cs_last, (SUBLANES, LANES)))

    lane_id = lax.broadcasted_iota(jnp.int32, (tc, LANES), 1)

    def expand(v, e):
        hi = v.astype(BF16).astype(F32)
        r1 = v - hi
        mid = r1.astype(BF16).astype(F32)
        lo = r1 - mid
        lanes = lane_id[:v.shape[0]]
        packed = jnp.where(lanes < SSM_HEADS, hi,
                           jnp.where(lanes < 2 * SSM_HEADS, pltpu.roll(mid, SSM_HEADS, 1),
                                     pltpu.roll(lo, 2 * SSM_HEADS, 1)))
        return jnp.dot(packed.astype(BF16), e, preferred_element_type=F32)

    e_mat = e_ref[...]
    dt_full = expand(dt, e_mat)
    ecs_full = expand(ecs, e_mat)
    toend_full = expand(to_end, e_mat)
    cdecay_full = expand(cdecay, e_mat)[0:1, :]
    cs_col = expand(cs, e2_ref[...])
    cs_row = cs.T

    xdt = xs * dt_full
    xdt_b = xdt.astype(BF16)
    xw_b = (xdt * toend_full).astype(BF16)
    z = zx_ref[0, :, :D_INNER]

    tri_mask = lax.broadcasted_iota(jnp.int32, (tc, tc), 0) >= lax.broadcasted_iota(jnp.int32, (tc, tc), 1)
    lane_lo = lax.broadcasted_iota(jnp.int32, (tc, LANES), 1) < SSM_HEAD_DIM

    for g in range(SSM_GROUPS):
        gs = slice(g * gw, (g + 1) * gw)
        b_g = bm[:, g * D_STATE:(g + 1) * D_STATE]
        c_g = cm[:, g * D_STATE:(g + 1) * D_STATE].astype(BF16)
        cb = lax.dot_general(c_g, b_g.astype(BF16), (((1,), (1,)), ((), ())), preferred_element_type=F32)
        bt_g = b_g.T.astype(BF16)
        st_old = state_scr[g]
        y_off = jnp.dot(c_g, st_old.astype(BF16), preferred_element_type=F32)
        s_new = jnp.dot(bt_g, xw_b[:, gs], preferred_element_type=F32)
        state_scr[g] = st_old * cdecay_full[:, gs] + s_new
        pairs = []
        for j in range(n_pairs):
            x_pair = xdt_b[:, g * gw + j * LANES:g * gw + (j + 1) * LANES]
            acc = None
            for r2 in range(2):
                h = g * HEADS_PER_GROUP + 2 * j + r2
                seg = cs_col[:, h * LANES:(h + 1) * LANES] - cs_row[h:h + 1, :]
                m = (jnp.exp(jnp.where(tri_mask, seg, NEG_BIG)) * cb).astype(BF16)
                keep = lane_lo if r2 == 0 else jnp.logical_not(lane_lo)
                rhs = jnp.where(keep, x_pair, jnp.zeros_like(x_pair))
                part = jnp.dot(m, rhs, preferred_element_type=F32)
                acc = part if acc is None else acc + part
            pairs.append(acc)
        y = jnp.concatenate(pairs, axis=1) + y_off * ecs_full[:, gs] + xs[:, gs] * dfull_ref[:, gs]
        gated = y * _silu(z[:, gs])
        ms = jnp.mean(gated * gated, axis=-1, keepdims=True)
        g_ref[0, :, gs] = (gated * lax.rsqrt(ms + GATED_NORM_EPS) * gnorm_ref[:, gs]).astype(g_ref.dtype)

    @pl.when(ci == nc - 1)
    def _fin():
        for g in range(SSM_GROUPS):
            for j in range(n_pairs):
                st = state_scr[g, :, j * LANES:(j + 1) * LANES].T
                ht_ref[0, pl.ds(g * HEADS_PER_GROUP + 2 * j, 2)] = st.reshape(2, SSM_HEAD_DIM, D_STATE)


def _ssd_constants(tc):
    tri = (jnp.arange(tc)[:, None] >= jnp.arange(tc)[None, :]).astype(BF16)
    head = jnp.arange(LANES)

    def expansion(width):
        cols = jnp.arange(SSM_HEADS * width) // width
        piece = head % SSM_HEADS
        return ((piece[:, None] == cols[None, :]) & (head[:, None] < 3 * SSM_HEADS)).astype(BF16)

    return tri, expansion(SSM_HEAD_DIM), expansion(LANES)


def _ssd(zx, dtr, cbuf8, h0, conv_w, conv_b, dt_bias, a_log, d_skip, gate_norm, valid):
    b, lp, _ = zx.shape
    tc = SSD_CHUNK
    nc = lp // tc
    tri, e_mat, e2_mat = _ssd_constants(tc)
    pad = LANES - SSM_HEADS
    dtb = jnp.pad(dt_bias, (0, pad)).reshape(1, LANES)
    alog = jnp.pad(a_log, (0, pad)).reshape(1, LANES)
    dfull = jnp.repeat(d_skip, SSM_HEAD_DIM).reshape(1, D_INNER)
    const = lambda *shape: pl.BlockSpec(shape, lambda bi, ci: (0,) * len(shape))
    return pl.pallas_call(
        functools.partial(_ssd_kernel, tc=tc, valid=valid),
        grid=(b, nc),
        in_specs=[pl.BlockSpec((1, tc, ZX_DIM), lambda bi, ci: (bi, ci, 0)),
                  pl.BlockSpec((1, tc, LANES), lambda bi, ci: (bi, ci, 0)),
                  pl.BlockSpec((1, SUBLANES, CONV_DIM), lambda bi, ci: (bi, 0, 0)),
                  pl.BlockSpec((1, SSM_HEADS, SSM_HEAD_DIM, D_STATE), lambda bi, ci: (bi, 0, 0, 0)),
                  const(CONV_W, CONV_DIM), const(1, CONV_DIM), const(1, LANES), const(1, LANES),
                  const(1, D_INNER), const(1, D_INNER),
                  const(tc, tc), const(LANES, D_INNER), const(LANES, SSM_HEADS * LANES)],
        out_specs=[pl.BlockSpec((1, tc, D_INNER), lambda bi, ci: (bi, ci, 0)),
                   pl.BlockSpec((1, SSM_HEADS, SSM_HEAD_DIM, D_STATE), lambda bi, ci: (bi, 0, 0, 0))],
        out_shape=[jax.ShapeDtypeStruct((b, lp, D_INNER), BF16),
                   jax.ShapeDtypeStruct((b, SSM_HEADS, SSM_HEAD_DIM, D_STATE), F32)],
        scratch_shapes=[pltpu.VMEM((SSM_GROUPS, D_STATE, HEADS_PER_GROUP * SSM_HEAD_DIM), F32),
                        pltpu.VMEM((SUBLANES, CONV_DIM), F32)],
        compiler_params=_params("parallel", "arbitrary"),
        name="ssd",
    )(zx, dtr, cbuf8, h0, conv_w, conv_b.reshape(1, CONV_DIM), dtb, alog, dfull,
      gate_norm.reshape(1, D_INNER), tri, e_mat, e2_mat)


def _mamba_layer(x, conv_buf, ssm_state, p, j):
    b, l, d = x.shape
    t = b * l
    w_in = p['ssm_w_in'][j]
    w_zx = w_in[:, :ZX_DIM].astype(BF16)
    w_dt = jnp.pad(w_in[:, ZX_DIM:], ((0, 0), (0, LANES - SSM_HEADS))).astype(BF16)
    zx, dtr = _norm_matmul(x.reshape(t, d), p['ssm_norm'][j], [w_zx], w_small=w_dt)
    zx = zx.reshape(b, l, ZX_DIM)
    dtr = dtr.reshape(b, l, LANES)
    xbc_raw = zx[:, :, D_INNER:]
    new_conv = jnp.concatenate([conv_buf, xbc_raw[:, max(l - (CONV_W - 1), 0):]], axis=1)[:, -(CONV_W - 1):]
    lp = -(-l // SSD_CHUNK) * SSD_CHUNK
    valid = None
    if lp != l:
        valid = l
        zx = jnp.pad(zx, ((0, 0), (0, lp - l), (0, 0)))
        dtr = jnp.pad(dtr, ((0, 0), (0, lp - l), (0, 0)))
    cbuf8 = jnp.pad(conv_buf, ((0, 0), (SUBLANES - (CONV_W - 1), 0), (0, 0)))
    gated, new_state = _ssd(zx, dtr, cbuf8, ssm_state, p['ssm_conv_w'][j], p['ssm_conv_b'][j],
                            p['ssm_dt_bias'][j], p['ssm_a_log'][j], p['ssm_d'][j], p['ssm_gate_norm'][j], valid)
    gated = gated[:, :l].reshape(t, D_INNER)
    x_new = _matmul_res(gated, p['ssm_w_out'][j].astype(BF16), x.reshape(t, d))
    return x_new.reshape(b, l, d), new_conv, new_state


def _router_kernel(x_ref, g_ref, wr_ref, low_ref, h_ref, route_ref, cnt_ref, carry_scr):
    @pl.when(pl.program_id(0) == 0)
    def _():
        carry_scr[...] = jnp.zeros_like(carry_scr)

    x = x_ref[...]
    ms = jnp.mean(x * x, axis=-1, keepdims=True)
    h = x * lax.rsqrt(ms + RMS_EPS) * g_ref[...]
    h1 = h.astype(BF16)
    h_ref[...] = h1
    h2 = (h - h1.astype(F32)).astype(BF16)
    logits = jnp.dot(jnp.concatenate([h1, h1, h2], axis=1), wr_ref[...], preferred_element_type=F32)
    tm = x.shape[0]
    lane = lax.broadcasted_iota(jnp.int32, (tm, LANES), 1)
    gl = jnp.where(lane < N_EGROUPS, logits, NEG_BIG)
    gmax = jnp.max(gl, axis=-1, keepdims=True)
    g_sel = jnp.min(jnp.where(gl == gmax, lane, LANES), axis=-1, keepdims=True)
    p_group = 1.0 / jnp.sum(jnp.exp(gl - gmax), axis=-1, keepdims=True)
    lo = N_EGROUPS + EXPERTS_PER_GROUP * g_sel
    el = jnp.where((lane >= lo) & (lane < lo + EXPERTS_PER_GROUP), logits, NEG_BIG)
    v1 = jnp.max(el, axis=-1, keepdims=True)
    i1 = jnp.min(jnp.where(el == v1, lane, LANES), axis=-1, keepdims=True)
    el2 = jnp.where(lane == i1, NEG_BIG, el)
    v2 = jnp.max(el2, axis=-1, keepdims=True)
    i2 = jnp.min(jnp.where(el2 == v2, lane, LANES), axis=-1, keepdims=True)
    e2 = jnp.exp(v2 - v1)
    gate1 = p_group / (1.0 + e2)
    gate2 = p_group * e2 / (1.0 + e2)
    hit1 = lane == i1
    hit2 = lane == i2
    sel = jnp.where(hit1 | hit2, 1.0, 0.0)
    before = jnp.dot(low_ref[...], sel.astype(BF16), preferred_element_type=F32) + carry_scr[0:1, :]
    rank1 = jnp.sum(jnp.where(hit1, before, 0.0), axis=-1, keepdims=True)
    rank2 = jnp.sum(jnp.where(hit2, before, 0.0), axis=-1, keepdims=True)
    carry_scr[...] = carry_scr[...] + jnp.sum(sel, axis=0, keepdims=True)
    cnt_ref[...] = carry_scr[...]
    vals = ((i1 - N_EGROUPS).astype(F32), (i2 - N_EGROUPS).astype(F32), gate1, gate2, rank1, rank2)
    out = jnp.zeros((tm, LANES), F32)
    for k, v in enumerate(vals):
        out = jnp.where(lane == k, v, out)
    route_ref[...] = out


def _router(x, gain, w_group, w_expert, layer, tm=512):
    t, d = x.shape
    tm = min(tm, t)
    wr = jnp.concatenate([w_group, w_expert], axis=2)
    wr = jnp.pad(wr, ((0, 0), (0, 0), (0, LANES - wr.shape[2])))
    w1 = wr.astype(BF16)
    w2 = (wr - w1.astype(F32)).astype(BF16)
    wr3 = jnp.concatenate([w1, w2, w1], axis=1)[layer]
    strict_lower = (jnp.arange(tm)[:, None] > jnp.arange(tm)[None, :]).astype(BF16)
    return pl.pallas_call(
        _router_kernel,
        grid=(t // tm,),
        in_specs=[pl.BlockSpec((tm, d), lambda i: (i, 0)), _resident((1, d)), _resident((3 * d, LANES)),
                  _resident((tm, tm))],
        out_specs=[pl.BlockSpec((tm, d), lambda i: (i, 0)),
                   pl.BlockSpec((tm, LANES), lambda i: (i, 0)),
                   pl.BlockSpec((SUBLANES, LANES), lambda i: (0, 0))],
        out_shape=[jax.ShapeDtypeStruct((t, d), BF16), jax.ShapeDtypeStruct((t, LANES), F32),
                   jax.ShapeDtypeStruct((SUBLANES, LANES), F32)],
        scratch_shapes=[pltpu.VMEM((SUBLANES, LANES), F32)],
        compiler_params=_params("arbitrary"),
        name="moe_router",
    )(x, gain.reshape(1, d), wr3, strict_lower)


def _grouped_mlp_kernel(be_ref, nu_ref, xs_ref, wg_ref, wu_ref, wd_ref, o_ref, wg_s, wu_s, wd_s):
    b = pl.program_id(0)
    changed = jnp.logical_or(b == 0, be_ref[b] != be_ref[jnp.maximum(b - 1, 0)])

    @pl.when(changed)
    def _():
        wg_s[...] = wg_ref[0, 0].astype(BF16)
        wu_s[...] = wu_ref[0, 0].astype(BF16)
        wd_s[...] = wd_ref[0, 0].astype(BF16)

    @pl.when(b < nu_ref[0])
    def _():
        x = xs_ref[...]
        gate = jnp.dot(x, wg_s[...], preferred_element_type=F32)
        up = jnp.dot(x, wu_s[...], preferred_element_type=F32)
        hid = (_silu(gate) * up).astype(BF16)
        o_ref[...] = jnp.dot(hid, wd_s[...], preferred_element_type=F32).astype(o_ref.dtype)

    @pl.when(b >= nu_ref[0])
    def _():
        o_ref[...] = jnp.zeros_like(o_ref)


def _grouped_mlp(xs, block_expert, n_used, w_gate, w_up, w_down, layer):
    n_slots, d = xs.shape
    ff = w_gate.shape[3]
    n_blocks = n_slots // MOE_ROWS
    grid_spec = pltpu.PrefetchScalarGridSpec(
        num_scalar_prefetch=2,
        grid=(n_blocks,),
        in_specs=[pl.BlockSpec((MOE_ROWS, d), lambda b, be, nu: (b, 0)),
                  pl.BlockSpec((1, 1, d, ff), lambda b, be, nu: (layer, be[b], 0, 0)),
                  pl.BlockSpec((1, 1, d, ff), lambda b, be, nu: (layer, be[b], 0, 0)),
                  pl.BlockSpec((1, 1, ff, d), lambda b, be, nu: (layer, be[b], 0, 0))],
        out_specs=pl.BlockSpec((MOE_ROWS, d), lambda b, be, nu: (b, 0)),
        scratch_shapes=[pltpu.VMEM((d, ff), BF16), pltpu.VMEM((d, ff), BF16), pltpu.VMEM((ff, d), BF16)],
    )
    return pl.pallas_call(
        _grouped_mlp_kernel,
        grid_spec=grid_spec,
        out_shape=jax.ShapeDtypeStruct((n_slots, d), BF16),
        compiler_params=_params("arbitrary"),
        name="moe_experts",
    )(block_expert, n_used, xs, w_gate, w_up, w_down)


def _combine_kernel(x_ref, y0_ref, y1_ref, route_ref, gain_ref, o_ref, *, final):
    r = route_ref[...]
    xn = x_ref[...] + r[:, 2:3] * y0_ref[...].astype(F32) + r[:, 3:4] * y1_ref[...].astype(F32)
    if final:
        ms = jnp.mean(xn * xn, axis=-1, keepdims=True)
        xn = xn * lax.rsqrt(ms + RMS_EPS) * gain_ref[...]
    o_ref[...] = xn


def _combine(x, y01, route, final_gain, tm=512):
    t, d = x.shape
    tm = min(tm, t)
    final = final_gain is not None
    gain = (final_gain if final else jnp.ones((d,), F32)).reshape(1, d)
    row = pl.BlockSpec((tm, d), lambda i: (i, 0))
    return pl.pallas_call(
        functools.partial(_combine_kernel, final=final),
        grid=(t // tm,),
        in_specs=[row, row, pl.BlockSpec((tm, d), lambda i: (i + t // tm, 0)),
                  pl.BlockSpec((tm, LANES), lambda i: (i, 0)), _resident((1, d))],
        out_specs=row,
        out_shape=jax.ShapeDtypeStruct((t, d), F32),
        compiler_params=_params("parallel"),
        name="moe_combine",
    )(x, y01, y01, route, gain)


def _moe_layer(x, p, i, final_gain=None):
    t, d = x.shape
    hb, route, cnt = _router(x, p['moe_norm'][i], p['moe_w_group'], p['moe_w_expert'], i)
    a = t * TOP_K
    r = MOE_ROWS
    i32 = jnp.int32
    e_idx = route[:, :TOP_K].astype(i32)
    rank = route[:, 4:4 + TOP_K].astype(i32)
    counts = cnt[0, N_EGROUPS:N_EGROUPS + N_EXPERTS].astype(i32)
    padded = (counts + r - 1) // r * r
    pad_end = jnp.cumsum(padded)
    pad_start = pad_end - padded
    start = jnp.cumsum(counts) - counts
    dest = (pad_start[e_idx] + rank).T.reshape(a)
    n_blocks = (a + N_EXPERTS * (r - 1)) // r
    n_used = (pad_end[-1] // r).astype(i32)
    blk = jnp.arange(n_blocks, dtype=i32)
    block_expert = jnp.minimum(jnp.searchsorted(pad_end, blk * r, side='right'), N_EXPERTS - 1).astype(i32)
    order = jnp.argsort(e_idx.reshape(a), stable=True).astype(i32)
    off = blk[:, None] * r + jnp.arange(r, dtype=i32)[None, :] - pad_start[block_expert][:, None]
    live = (off < counts[block_expert][:, None]) & (blk < n_used)[:, None]
    src = jnp.clip(start[block_expert][:, None] + off, 0, a - 1)
    spread = (blk[:, None] * r + jnp.arange(r, dtype=i32)[None, :]) % t
    slot_tok = jnp.where(live, order[src.reshape(-1)].reshape(n_blocks, r) // TOP_K, spread).reshape(n_blocks * r)
    last_e = block_expert[jnp.maximum(n_used - 1, 0)]
    block_expert = jnp.where(blk < n_used, block_expert, last_e)
    xs = hb[slot_tok]
    yb = _grouped_mlp(xs, block_expert, n_used.reshape(1), p['moe_w_gate'], p['moe_w_up'], p['moe_w_down'], i)
    return _combine(x, yb[dest], route, final_gain)


def _attn_finish(o_halves, tq, lam, subln, out_scale):
    od = o_halves[:tq] - lam * o_halves[tq:]
    ms = jnp.mean(od * od, axis=-1, keepdims=True)
    return od * lax.rsqrt(ms + RMS_EPS) * subln * out_scale


def _attn_kernel(lam_ref, sub_ref, q_ref, k_ref, v_ref, o_ref, kb_scr, vb_scr, *, tq, tk, out_scale):
    qi = pl.program_id(2)

    @pl.when(qi == 0)
    def _():
        kb_scr[...] = k_ref[0].astype(BF16)
        vb_scr[:, :HEAD_W] = v_ref[0].astype(BF16)
        vb_scr[:, HEAD_W:] = jnp.ones((vb_scr.shape[0], HEAD_W), BF16)

    q = q_ref[0] * (ATTN_SCALE * LOG2E)
    lane = lax.broadcasted_iota(jnp.int32, (tq, HEAD_W), 1)
    qs = jnp.concatenate([jnp.where(lane < HALF_DIM, q, 0.0), jnp.where(lane >= HALF_DIM, q, 0.0)],
                         axis=0).astype(BF16)

    def scores(j):
        kblk = kb_scr[pl.ds(pl.multiple_of(j * tk, tk), tk), :]
        return lax.dot_general(qs, kblk, (((1,), (1,)), ((), ())), preferred_element_type=F32)

    def softmax_pv(j, s, m, acc, masked):
        vblk = vb_scr[pl.ds(pl.multiple_of(j * tk, tk), tk), :]
        if masked:
            rq = lax.broadcasted_iota(jnp.int32, (2 * tq, tk), 0)
            rq = jnp.where(rq >= tq, rq - tq, rq) + qi * tq
            ck = lax.broadcasted_iota(jnp.int32, (2 * tq, tk), 1) + j * tk
            s = jnp.where(ck <= rq, s, NEG_BIG)
        m_new = jnp.maximum(m, jnp.max(s, axis=-1, keepdims=True))
        pexp = jnp.exp2((s - m_new).astype(BF16))
        acc = acc * jnp.exp2(m - m_new) + jnp.dot(pexp, vblk, preferred_element_type=F32)
        return m_new, acc

    def body(j, carry):
        s, m, acc = carry
        s_next = scores(j + 1)
        m, acc = softmax_pv(j, s, m, acc, False)
        return s_next, m, acc

    init = (scores(0), jnp.full((2 * tq, 1), NEG_BIG, F32), jnp.zeros((2 * tq, 2 * HEAD_W), F32))
    n_full = (qi * tq) // tk
    s, m, acc = lax.fori_loop(0, n_full, body, init)
    m, acc = softmax_pv(n_full, s, m, acc, True)
    o = acc[:, :HEAD_W] / acc[:, HEAD_W:]
    o_ref[0] = _attn_finish(o, tq, lam_ref[...], sub_ref[...], out_scale).astype(o_ref.dtype)


def _prompt_attention(q, k, v, lam, subln, out_scale, b, s):
    d = q.shape[1]
    tq = min(ATTN_TQ, s)
    tk = min(ATTN_TK, s)
    q3, k3, v3 = (a.reshape(b, s, d) for a in (q, k, v))
    lam_row = jnp.full((1, HEAD_W), lam, F32)
    out = pl.pallas_call(
        functools.partial(_attn_kernel, tq=tq, tk=tk, out_scale=out_scale),
        grid=(b, ATTN_HEADS, s // tq),
        in_specs=[pl.BlockSpec((1, HEAD_W), lambda bi, h, qi: (0, 0)),
                  pl.BlockSpec((1, HEAD_W), lambda bi, h, qi: (0, 0)),
                  pl.BlockSpec((1, tq, HEAD_W), lambda bi, h, qi: (bi, qi, h)),
                  pl.BlockSpec((1, s, HEAD_W), lambda bi, h, qi: (bi, 0, h)),
                  pl.BlockSpec((1, s, HEAD_W), lambda bi, h, qi: (bi, 0, h))],
        out_specs=pl.BlockSpec((1, tq, HEAD_W), lambda bi, h, qi: (bi, qi, h)),
        out_shape=jax.ShapeDtypeStruct((b, s, d), BF16),
        scratch_shapes=[pltpu.VMEM((s, HEAD_W), BF16), pltpu.VMEM((s, 2 * HEAD_W), BF16)],
        compiler_params=_params("parallel", "parallel", "arbitrary"),
        name="prompt_attention",
    )(lam_row, subln.reshape(1, HEAD_W), q3, k3, v3)
    return out.reshape(b * s, d)


def _paged_kernel(pt_ref, lam_ref, sub_ref, qm_ref, kn_ref, vn_ref, *rest, n_pg, tqn, out_scale):
    k_refs = rest[:n_pg]
    v_refs = rest[n_pg:2 * n_pg]
    o_ref = rest[2 * n_pg]
    m_scr, l_scr, acc_scr = rest[2 * n_pg + 1:]
    si = pl.program_id(1)
    ns = pl.num_programs(1)
    rows = ATTN_HEADS * 2 * tqn
    cols = PAGE_SIZE * ATTN_HEADS

    @pl.when(si == 0)
    def _():
        m_scr[...] = jnp.full_like(m_scr, NEG_BIG)
        l_scr[...] = jnp.zeros_like(l_scr)
        acc_scr[...] = jnp.zeros_like(acc_scr)

    qm = qm_ref[0].astype(BF16)
    nt = (((1,), (1,)), ((), ()))
    same_head = (lax.broadcasted_iota(jnp.int32, (rows, cols), 0) // (2 * tqn)
                 == lax.broadcasted_iota(jnp.int32, (rows, cols), 1) % ATTN_HEADS)
    scores = []
    for g in range(n_pg):
        k2 = k_refs[g][0].reshape(cols, HEAD_W).astype(BF16)
        s = lax.dot_general(qm, k2, nt, preferred_element_type=F32)
        scores.append(jnp.where(same_head, s, NEG_BIG))
    m_old = m_scr[...]
    m_new = m_old
    for s in scores:
        m_new = jnp.maximum(m_new, jnp.max(s, axis=-1, keepdims=True))
    corr = jnp.exp(m_old - m_new)
    l = l_scr[...] * corr
    acc = acc_scr[...] * corr
    for g in range(n_pg):
        pexp = jnp.exp(scores[g] - m_new)
        l = l + jnp.sum(pexp, axis=-1, keepdims=True)
        v2 = v_refs[g][0].reshape(cols, HEAD_W).astype(BF16)
        acc = acc + jnp.dot(pexp.astype(BF16), v2, preferred_element_type=F32)
    m_scr[...] = m_new
    l_scr[...] = l
    acc_scr[...] = acc

    @pl.when(si == ns - 1)
    def _():
        npad = kn_ref.shape[1]
        s = lax.dot_general(qm, kn_ref[0].astype(BF16), nt, preferred_element_type=F32)
        r = lax.broadcasted_iota(jnp.int32, (rows, npad), 0)
        c = lax.broadcasted_iota(jnp.int32, (rows, npad), 1)
        ok = (c < tqn * ATTN_HEADS) & (c % ATTN_HEADS == r // (2 * tqn)) & (c // ATTN_HEADS <= r % tqn)
        s = jnp.where(ok, s, NEG_BIG)
        m1 = jnp.maximum(m_new, jnp.max(s, axis=-1, keepdims=True))
        pexp = jnp.exp(s - m1)
        corr1 = jnp.exp(m_new - m1)
        l1 = l * corr1 + jnp.sum(pexp, axis=-1, keepdims=True)
        acc1 = acc * corr1 + jnp.dot(pexp.astype(BF16), vn_ref[0].astype(BF16), preferred_element_type=F32)
        o = acc1 / l1
        od = o - lam_ref[...] * pltpu.roll(o, rows - tqn, 0)
        ms = jnp.mean(od * od, axis=-1, keepdims=True)
        o_ref[0] = od * lax.rsqrt(ms + RMS_EPS) * sub_ref[...] * out_scale


def _paged_attention(q, k_new, v_new, cache_k, cache_v, page_table, lam, subln, out_scale):
    b, n_pages = page_table.shape
    tqn = q.shape[0] // b
    n_pg = PAGES_PER_STEP
    rows = ATTN_HEADS * 2 * tqn
    q4 = q.reshape(b, tqn, ATTN_HEADS, HEAD_W).transpose(0, 2, 1, 3) * ATTN_SCALE
    lane = jnp.arange(HEAD_W)
    qm = jnp.stack([jnp.where(lane < HALF_DIM, q4, 0.0), jnp.where(lane >= HALF_DIM, q4, 0.0)], axis=2)
    qm = qm.reshape(b, rows, HEAD_W)
    npad = LANES
    kn = jnp.pad(k_new.reshape(b, tqn * ATTN_HEADS, HEAD_W), ((0, 0), (0, npad - tqn * ATTN_HEADS), (0, 0)))
    vn = jnp.pad(v_new.reshape(b, tqn * ATTN_HEADS, HEAD_W), ((0, 0), (0, npad - tqn * ATTN_HEADS), (0, 0)))
    lam_row = jnp.full((1, HEAD_W), lam, F32)

    def page_spec(g):
        return pl.BlockSpec((1, PAGE_SIZE, ATTN_HEADS, HEAD_W),
                            lambda bi, si, pt: (pt[bi, si * n_pg + g], 0, 0, 0))

    per_seq = lambda r: pl.BlockSpec((1, r, HEAD_W), lambda bi, si, pt: (bi, 0, 0))
    grid_spec = pltpu.PrefetchScalarGridSpec(
        num_scalar_prefetch=1,
        grid=(b, n_pages // n_pg),
        in_specs=[pl.BlockSpec((1, HEAD_W), lambda bi, si, pt: (0, 0)),
                  pl.BlockSpec((1, HEAD_W), lambda bi, si, pt: (0, 0)),
                  per_seq(rows), per_seq(npad), per_seq(npad)]
                 + [page_spec(g) for g in range(n_pg)] + [page_spec(g) for g in range(n_pg)],
        out_specs=per_seq(rows),
        scratch_shapes=[pltpu.VMEM((rows, 1), F32), pltpu.VMEM((rows, 1), F32), pltpu.VMEM((rows, HEAD_W), F32)],
    )
    o = pl.pallas_call(
        functools.partial(_paged_kernel, n_pg=n_pg, tqn=tqn, out_scale=out_scale),
        grid_spec=grid_spec,
        out_shape=jax.ShapeDtypeStruct((b, rows, HEAD_W), F32),
        compiler_params=_params("parallel", "arbitrary"),
        name="paged_attention",
    )(page_table, lam_row, subln.reshape(1, HEAD_W), qm, kn, vn, *([cache_k] * n_pg), *([cache_v] * n_pg))
    o = o.reshape(b, ATTN_HEADS, 2, tqn, HEAD_W)[:, :, 0]
    return o.transpose(0, 2, 1, 3).reshape(b * tqn, ATTN_HEADS * HEAD_W).astype(BF16)


def _trunk(x, conv_buf, ssm_state, p, attention):
    b, l, d = x.shape
    t = b * l
    x, new_conv, new_state = _mamba_layer(x, conv_buf, ssm_state, p, 0)
    x = _moe_layer(x.reshape(t, d), p, 0)
    k, v = _norm_matmul(x, p['kv_norm'], [p['w_k'].astype(BF16), p['w_v'].astype(BF16)])
    layer = N_A_LAYERS
    lam_init = 0.8 - 0.6 * math.exp(-0.3 * layer)
    lq = p['attn_lambda'][0]
    lam = jnp.exp(jnp.sum(lq[0] * lq[1])) - jnp.exp(jnp.sum(lq[2] * lq[3])) + lam_init
    q, = _norm_matmul(x, p['attn_norm'][0], [p['attn_w_q'][0].astype(BF16)])
    o = attention(q, k, v, lam, p['attn_subln'][0], 1.0 - lam_init)
    x = _matmul_res(o, p['attn_w_o'][0].astype(BF16), x)
    y = _moe_layer(x, p, layer, final_gain=p['final_norm'])
    kv_shape = (b, l, ATTN_HEADS, HEAD_W)
    return y.reshape(b, l, d), new_conv[None], new_state[None], k.reshape(kv_shape), v.reshape(kv_shape)


def kernel(x_prompt, x_sample, state_ssm, state_conv, cache_k, cache_v, page_table, ssm_norm, ssm_w_in, ssm_conv_w, ssm_conv_b, ssm_dt_bias, ssm_a_log, ssm_d, ssm_gate_norm, ssm_w_out, kv_norm, w_k, w_v, attn_norm, attn_w_q, attn_lambda, attn_subln, attn_w_o, moe_norm, moe_w_group, moe_w_expert, moe_w_gate, moe_w_up, moe_w_down, final_norm):
    p = dict(ssm_norm=ssm_norm, ssm_w_in=ssm_w_in, ssm_conv_w=ssm_conv_w, ssm_conv_b=ssm_conv_b,
             ssm_dt_bias=ssm_dt_bias, ssm_a_log=ssm_a_log, ssm_d=ssm_d, ssm_gate_norm=ssm_gate_norm,
             ssm_w_out=ssm_w_out, kv_norm=kv_norm, w_k=w_k, w_v=w_v, attn_norm=attn_norm,
             attn_w_q=attn_w_q, attn_lambda=attn_lambda, attn_subln=attn_subln, attn_w_o=attn_w_o,
             moe_norm=moe_norm, moe_w_group=moe_w_group, moe_w_expert=moe_w_expert,
             moe_w_gate=moe_w_gate, moe_w_up=moe_w_up, moe_w_down=moe_w_down, final_norm=final_norm)
    bp, sp, _ = x_prompt.shape
    conv0 = jnp.zeros((bp, CONV_W - 1, CONV_DIM), F32)
    ssm0 = jnp.zeros((bp, SSM_HEADS, SSM_HEAD_DIM, D_STATE), F32)
    prompt_attn = functools.partial(_prompt_attention, b=bp, s=sp)
    y_p, conv_p, ssm_p, k_p, v_p = _trunk(x_prompt, conv0, ssm0, p, prompt_attn)

    def sample_attn(q, k, v, lam, subln, out_scale):
        return _paged_attention(q, k, v, cache_k, cache_v, page_table, lam, subln, out_scale)

    y_s, conv_s, ssm_s, k_s, v_s = _trunk(x_sample, state_conv[0], state_ssm[0], p, sample_attn)
    return (y_p, y_s, ssm_p, conv_p, k_p, v_p, ssm_s, conv_s, k_s, v_s)
```

```python
import functools
import math

import jax
import jax.numpy as jnp
from jax import lax
from jax.experimental import pallas as pl
from jax.experimental.pallas import tpu as pltpu

F32 = jnp.float32
BF16 = jnp.bfloat16

D_MODEL = 1024
DEPTH = 2
N_A_LAYERS = DEPTH // 2
D_INNER = 2 * D_MODEL
SSM_HEAD_DIM = 64
SSM_HEADS = D_INNER // SSM_HEAD_DIM
SSM_GROUPS = 4
HEADS_PER_GROUP = SSM_HEADS // SSM_GROUPS
D_STATE = 128
CONV_W = 4
CONV_DIM = D_INNER + 2 * SSM_GROUPS * D_STATE
ZX_DIM = D_INNER + CONV_DIM
SSD_CHUNK = 128
GATED_NORM_EPS = 1e-5
HALF_DIM = 64
ATTN_HEADS = D_MODEL // (2 * HALF_DIM)
HEAD_W = 2 * HALF_DIM
ATTN_SCALE = HALF_DIM ** -0.5
N_EGROUPS = 4
EXPERTS_PER_GROUP = 8
N_EXPERTS = N_EGROUPS * EXPERTS_PER_GROUP
TOP_K = 2
EXPERT_FF = D_MODEL // 2
RMS_EPS = 1e-6
PAGE_SIZE = 128

LANES = 128
SUBLANES = 8
MOE_ROWS = 512
MOE_ROWS_SMALL = 128
ATTN_TQ = 256
ATTN_TK = 512
PAGES_PER_STEP = 16
COL_CHUNK = 1024
LOG2E = 1.4426950408889634
NEG_BIG = -1e30
VMEM_LIMIT = 56 * 1024 * 1024


def _params(*sem):
    return pltpu.CompilerParams(dimension_semantics=sem, vmem_limit_bytes=VMEM_LIMIT)


def _silu(x):
    return x * jax.nn.sigmoid(x)


def _split3(x):
    hi = x.astype(BF16)
    r1 = x - hi.astype(F32)
    mid = r1.astype(BF16)
    lo = (r1 - mid.astype(F32)).astype(BF16)
    return jnp.concatenate([hi, mid, lo], axis=1)


def _norm_matmul_kernel(*refs, n_w, has_small):
    x_ref, g_ref = refs[0], refs[1]
    w_refs = refs[2:2 + n_w]
    pos = 2 + n_w
    ws_ref = refs[pos] if has_small else None
    pos += int(has_small)
    o_refs = refs[pos:pos + n_w]
    pos += n_w
    os_ref = refs[pos] if has_small else None

    x = x_ref[...]
    ms = jnp.mean(x * x, axis=-1, keepdims=True)
    h = (x * lax.rsqrt(ms + RMS_EPS) * g_ref[...]).astype(BF16)
    if has_small:
        os_ref[...] = jnp.dot(h, ws_ref[...], preferred_element_type=F32)
    for w_ref, o_ref in zip(w_refs, o_refs):
        n = w_ref.shape[1]
        step = min(COL_CHUNK, n)
        for c in range(0, n, step):
            o_ref[:, c:c + step] = jnp.dot(h, w_ref[:, c:c + step], preferred_element_type=F32)


def _resident(shape):
    return pl.BlockSpec(shape, lambda i: (0,) * len(shape), pipeline_mode=pl.Buffered(1))


def _norm_matmul(x, gain, weights, w_small=None, tm=512):
    t, d = x.shape
    n = weights[0].shape[1]
    tm = min(tm, t)
    n_w = len(weights)
    has_small = w_small is not None
    in_specs = [pl.BlockSpec((tm, d), lambda i: (i, 0)), _resident((1, d))]
    in_specs += [_resident((d, n))] * n_w
    out_shape = [jax.ShapeDtypeStruct((t, n), F32)] * n_w
    out_specs = [pl.BlockSpec((tm, n), lambda i: (i, 0))] * n_w
    args = [x, gain.reshape(1, d)] + list(weights)
    if has_small:
        in_specs.append(_resident((d, LANES)))
        out_shape.append(jax.ShapeDtypeStruct((t, LANES), F32))
        out_specs.append(pl.BlockSpec((tm, LANES), lambda i: (i, 0)))
        args.append(w_small)
    return pl.pallas_call(
        functools.partial(_norm_matmul_kernel, n_w=n_w, has_small=has_small),
        grid=(t // tm,),
        in_specs=in_specs, out_specs=out_specs, out_shape=out_shape,
        compiler_params=_params("parallel"),
        name="norm_matmul",
    )(*args)


def _matmul_res_kernel(a_ref, w_ref, r_ref, o_ref):
    o_ref[...] = r_ref[...] + jnp.dot(a_ref[...], w_ref[...], preferred_element_type=F32)


def _matmul_res(a, w, res, tm=512):
    t, k = a.shape
    n = w.shape[1]
    tm = min(tm, t)
    return pl.pallas_call(
        _matmul_res_kernel,
        grid=(t // tm,),
        in_specs=[pl.BlockSpec((tm, k), lambda i: (i, 0)), _resident((k, n)),
                  pl.BlockSpec((tm, n), lambda i: (i, 0))],
        out_specs=pl.BlockSpec((tm, n), lambda i: (i, 0)),
        out_shape=jax.ShapeDtypeStruct((t, n), F32),
        compiler_params=_params("parallel"),
        name="matmul_res",
    )(a, w, res)


def _ssd_kernel(zx_ref, dt_ref, cbuf_ref, h0_ref, convw_ref, convb_ref, dtb_ref, alog_ref, dfull_ref,
                gnorm_ref, tri_ref, e_ref, e2_ref, g_ref, ht_ref, state_scr, tail_scr, *, tc, valid):
    ci = pl.program_id(1)
    nc = pl.num_programs(1)
    gw = HEADS_PER_GROUP * SSM_HEAD_DIM
    n_pairs = HEADS_PER_GROUP // 2

    @pl.when(ci == 0)
    def _init():
        tail_scr[...] = cbuf_ref[0]
        for g in range(SSM_GROUPS):
            for j in range(n_pairs):
                h2 = h0_ref[0, pl.ds(g * HEADS_PER_GROUP + 2 * j, 2)]
                state_scr[g, :, j * LANES:(j + 1) * LANES] = h2.reshape(2 * SSM_HEAD_DIM, D_STATE).T

    xr = zx_ref[0, :, D_INNER:ZX_DIM]
    xcat = jnp.concatenate([tail_scr[...], xr], axis=0)
    conv = convb_ref[...] + convw_ref[CONV_W - 1:CONV_W, :] * xr
    for k in range(1, CONV_W):
        conv = conv + convw_ref[CONV_W - 1 - k:CONV_W - k, :] * xcat[SUBLANES - k:SUBLANES - k + tc]
    tail_scr[...] = xr[tc - SUBLANES:tc]
    xbc = _silu(conv)
    xs = xbc[:, :D_INNER]
    bm = xbc[:, D_INNER:D_INNER + SSM_GROUPS * D_STATE]
    cm = xbc[:, D_INNER + SSM_GROUPS * D_STATE:]

    dtl = dt_ref[0] + dtb_ref[...]
    dt = jnp.maximum(dtl, 0.0) + jnp.log1p(jnp.exp(-jnp.abs(dtl)))
    row = lax.broadcasted_iota(jnp.int32, (tc, LANES), 0)
    if valid is not None:
        dt = jnp.where(row < valid, dt, 0.0)
    la = dt * (-jnp.exp(alog_ref[...]))
    cs3 = jnp.dot(tri_ref[...], _split3(la), preferred_element_type=F32)
    cs = cs3[:, :LANES] + cs3[:, LANES:2 * LANES] + cs3[:, 2 * LANES:]
    cs_last = cs[tc - 1:tc, :]
    ecs = jnp.exp(cs)
    to_end = jnp.exp(cs_last - cs)
    cdecay = jnp.exp(jnp.broadcast_to(cs_last, (SUBLANES, LANES)))

    lane_id = lax.broadcasted_iota(jnp.int32, (tc, LANES), 1)

    def expand(v, e):
        hi = v.astype(BF16).astype(F32)
        r1 = v - hi
        mid = r1.astype(BF16).astype(F32)
        lo = r1 - mid
        lanes = lane_id[:v.shape[0]]
        packed = jnp.where(lanes < SSM_HEADS, hi,
                           jnp.where(lanes < 2 * SSM_HEADS, pltpu.roll(mid, SSM_HEADS, 1),
                                     pltpu.roll(lo, 2 * SSM_HEADS, 1)))
        return jnp.dot(packed.astype(BF16), e, preferred_element_type=F32)

    e_mat = e_ref[...]
    dt_full = expand(dt, e_mat)
    ecs_full = expand(ecs, e_mat)
    toend_full = expand(to_end, e_mat)
    cdecay_full = expand(cdecay, e_mat)[0:1, :]
    cs_col = expand(cs, e2_ref[...])
    cs_row = cs.T

    xdt = xs * dt_full
    xdt_b = xdt.astype(BF16)
    xw_b = (xdt * toend_full).astype(BF16)
    z = zx_ref[0, :, :D_INNER]

    tri_mask = lax.broadcasted_iota(jnp.int32, (tc, tc), 0) >= lax.broadcasted_iota(jnp.int32, (tc, tc), 1)
    lane_lo = lax.broadcasted_iota(jnp.int32, (tc, LANES), 1) < SSM_HEAD_DIM

    for g in range(SSM_GROUPS):
        gs = slice(g * gw, (g + 1) * gw)
        b_g = bm[:, g * D_STATE:(g + 1) * D_STATE]
        c_g = cm[:, g * D_STATE:(g + 1) * D_STATE].astype(BF16)
        cb = lax.dot_general(c_g, b_g.astype(BF16), (((1,), (1,)), ((), ())), preferred_element_type=F32)
        bt_g = b_g.T.astype(BF16)
        st_old = state_scr[g]
        y_off = jnp.dot(c_g, st_old.astype(BF16), preferred_element_type=F32)
        s_new = jnp.dot(bt_g, xw_b[:, gs], preferred_element_type=F32)
        state_scr[g] = st_old * cdecay_full[:, gs] + s_new
        pairs = []
        for j in range(n_pairs):
            x_pair = xdt_b[:, g * gw + j * LANES:g * gw + (j + 1) * LANES]
            acc = None
            for r2 in range(2):
                h = g * HEADS_PER_GROUP + 2 * j + r2
                seg = cs_col[:, h * LANES:(h + 1) * LANES] - cs_row[h:h + 1, :]
                m = (jnp.exp(jnp.where(tri_mask, seg, NEG_BIG)) * cb).astype(BF16)
                keep = lane_lo if r2 == 0 else jnp.logical_not(lane_lo)
                rhs = jnp.where(keep, x_pair, jnp.zeros_like(x_pair))
                part = jnp.dot(m, rhs, preferred_element_type=F32)
                acc = part if acc is None else acc + part
            pairs.append(acc)
        y = jnp.concatenate(pairs, axis=1) + y_off * ecs_full[:, gs] + xs[:, gs] * dfull_ref[:, gs]
        gated = y * _silu(z[:, gs])
        ms = jnp.mean(gated * gated, axis=-1, keepdims=True)
        g_ref[0, :, gs] = (gated * lax.rsqrt(ms + GATED_NORM_EPS) * gnorm_ref[:, gs]).astype(g_ref.dtype)

    @pl.when(ci == nc - 1)
    def _fin():
        for g in range(SSM_GROUPS):
            for j in range(n_pairs):
                st = state_scr[g, :, j * LANES:(j + 1) * LANES].T
                ht_ref[0, pl.ds(g * HEADS_PER_GROUP + 2 * j, 2)] = st.reshape(2, SSM_HEAD_DIM, D_STATE)


def _ssd_constants(tc):
    tri = (jnp.arange(tc)[:, None] >= jnp.arange(tc)[None, :]).astype(BF16)
    head = jnp.arange(LANES)

    def expansion(width):
        cols = jnp.arange(SSM_HEADS * width) // width
        piece = head % SSM_HEADS
        return ((piece[:, None] == cols[None, :]) & (head[:, None] < 3 * SSM_HEADS)).astype(BF16)

    return tri, expansion(SSM_HEAD_DIM), expansion(LANES)


def _ssd(zx, dtr, cbuf8, h0, conv_w, conv_b, dt_bias, a_log, d_skip, gate_norm, valid):
    b, lp, _ = zx.shape
    tc = SSD_CHUNK
    nc = lp // tc
    tri, e_mat, e2_mat = _ssd_constants(tc)
    pad = LANES - SSM_HEADS
    dtb = jnp.pad(dt_bias, (0, pad)).reshape(1, LANES)
    alog = jnp.pad(a_log, (0, pad)).reshape(1, LANES)
    dfull = jnp.repeat(d_skip, SSM_HEAD_DIM).reshape(1, D_INNER)
    const = lambda *shape: pl.BlockSpec(shape, lambda bi, ci: (0,) * len(shape))
    return pl.pallas_call(
        functools.partial(_ssd_kernel, tc=tc, valid=valid),
        grid=(b, nc),
        in_specs=[pl.BlockSpec((1, tc, ZX_DIM), lambda bi, ci: (bi, ci, 0)),
                  pl.BlockSpec((1, tc, LANES), lambda bi, ci: (bi, ci, 0)),
                  pl.BlockSpec((1, SUBLANES, CONV_DIM), lambda bi, ci: (bi, 0, 0)),
                  pl.BlockSpec((1, SSM_HEADS, SSM_HEAD_DIM, D_STATE), lambda bi, ci: (bi, 0, 0, 0)),
                  const(CONV_W, CONV_DIM), const(1, CONV_DIM), const(1, LANES), const(1, LANES),
                  const(1, D_INNER), const(1, D_INNER),
                  const(tc, tc), const(LANES, D_INNER), const(LANES, SSM_HEADS * LANES)],
        out_specs=[pl.BlockSpec((1, tc, D_INNER), lambda bi, ci: (bi, ci, 0)),
                   pl.BlockSpec((1, SSM_HEADS, SSM_HEAD_DIM, D_STATE), lambda bi, ci: (bi, 0, 0, 0))],
        out_shape=[jax.ShapeDtypeStruct((b, lp, D_INNER), BF16),
                   jax.ShapeDtypeStruct((b, SSM_HEADS, SSM_HEAD_DIM, D_STATE), F32)],
        scratch_shapes=[pltpu.VMEM((SSM_GROUPS, D_STATE, HEADS_PER_GROUP * SSM_HEAD_DIM), F32),
                        pltpu.VMEM((SUBLANES, CONV_DIM), F32)],
        compiler_params=_params("parallel", "arbitrary"),
        name="ssd",
    )(zx, dtr, cbuf8, h0, conv_w, conv_b.reshape(1, CONV_DIM), dtb, alog, dfull,
      gate_norm.reshape(1, D_INNER), tri, e_mat, e2_mat)


def _mamba_layer(x, conv_buf, ssm_state, p, j):
    b, l, d = x.shape
    t = b * l
    w_in = p['ssm_w_in'][j]
    w_zx = w_in[:, :ZX_DIM].astype(BF16)
    w_dt = jnp.pad(w_in[:, ZX_DIM:], ((0, 0), (0, LANES - SSM_HEADS))).astype(BF16)
    zx, dtr = _norm_matmul(x.reshape(t, d), p['ssm_norm'][j], [w_zx], w_small=w_dt)
    zx = zx.reshape(b, l, ZX_DIM)
    dtr = dtr.reshape(b, l, LANES)
    xbc_raw = zx[:, :, D_INNER:]
    new_conv = jnp.concatenate([conv_buf, xbc_raw[:, max(l - (CONV_W - 1), 0):]], axis=1)[:, -(CONV_W - 1):]
    lp = -(-l // SSD_CHUNK) * SSD_CHUNK
    valid = None
    if lp != l:
        valid = l
        zx = jnp.pad(zx, ((0, 0), (0, lp - l), (0, 0)))
        dtr = jnp.pad(dtr, ((0, 0), (0, lp - l), (0, 0)))
    cbuf8 = jnp.pad(conv_buf, ((0, 0), (SUBLANES - (CONV_W - 1), 0), (0, 0)))
    gated, new_state = _ssd(zx, dtr, cbuf8, ssm_state, p['ssm_conv_w'][j], p['ssm_conv_b'][j],
                            p['ssm_dt_bias'][j], p['ssm_a_log'][j], p['ssm_d'][j], p['ssm_gate_norm'][j], valid)
    gated = gated[:, :l].reshape(t, D_INNER)
    x_new = _matmul_res(gated, p['ssm_w_out'][j].astype(BF16), x.reshape(t, d))
    return x_new.reshape(b, l, d), new_conv, new_state


def _router_kernel(x_ref, g_ref, wr_ref, low_ref, h_ref, route_ref, cnt_ref, carry_scr):
    @pl.when(pl.program_id(0) == 0)
    def _():
        carry_scr[...] = jnp.zeros_like(carry_scr)

    x = x_ref[...]
    ms = jnp.mean(x * x, axis=-1, keepdims=True)
    h = x * lax.rsqrt(ms + RMS_EPS) * g_ref[...]
    h1 = h.astype(BF16)
    h_ref[...] = h1
    h2 = (h - h1.astype(F32)).astype(BF16)
    logits = jnp.dot(jnp.concatenate([h1, h1, h2], axis=1), wr_ref[...], preferred_element_type=F32)
    tm = x.shape[0]
    lane = lax.broadcasted_iota(jnp.int32, (tm, LANES), 1)
    gl = jnp.where(lane < N_EGROUPS, logits, NEG_BIG)
    gmax = jnp.max(gl, axis=-1, keepdims=True)
    g_sel = jnp.min(jnp.where(gl == gmax, lane, LANES), axis=-1, keepdims=True)
    p_group = 1.0 / jnp.sum(jnp.exp(gl - gmax), axis=-1, keepdims=True)
    lo = N_EGROUPS + EXPERTS_PER_GROUP * g_sel
    el = jnp.where((lane >= lo) & (lane < lo + EXPERTS_PER_GROUP), logits, NEG_BIG)
    v1 = jnp.max(el, axis=-1, keepdims=True)
    i1 = jnp.min(jnp.where(el == v1, lane, LANES), axis=-1, keepdims=True)
    el2 = jnp.where(lane == i1, NEG_BIG, el)
    v2 = jnp.max(el2, axis=-1, keepdims=True)
    i2 = jnp.min(jnp.where(el2 == v2, lane, LANES), axis=-1, keepdims=True)
    e2 = jnp.exp(v2 - v1)
    gate1 = p_group / (1.0 + e2)
    gate2 = p_group * e2 / (1.0 + e2)
    hit1 = lane == i1
    hit2 = lane == i2
    sel = jnp.where(hit1 | hit2, 1.0, 0.0)
    before = jnp.dot(low_ref[...], sel.astype(BF16), preferred_element_type=F32) + carry_scr[0:1, :]
    rank1 = jnp.sum(jnp.where(hit1, before, 0.0), axis=-1, keepdims=True)
    rank2 = jnp.sum(jnp.where(hit2, before, 0.0), axis=-1, keepdims=True)
    carry_scr[...] = carry_scr[...] + jnp.sum(sel, axis=0, keepdims=True)
    cnt_ref[...] = carry_scr[...]
    vals = ((i1 - N_EGROUPS).astype(F32), (i2 - N_EGROUPS).astype(F32), gate1, gate2, rank1, rank2)
    out = jnp.zeros((tm, LANES), F32)
    for k, v in enumerate(vals):
        out = jnp.where(lane == k, v, out)
    route_ref[...] = out


def _router(x, gain, w_group, w_expert, layer, tm=512):
    t, d = x.shape
    tm = min(tm, t)
    wr = jnp.concatenate([w_group, w_expert], axis=2)
    wr = jnp.pad(wr, ((0, 0), (0, 0), (0, LANES - wr.shape[2])))
    w1 = wr.astype(BF16)
    w2 = (wr - w1.astype(F32)).astype(BF16)
    wr3 = jnp.concatenate([w1, w2, w1], axis=1)[layer]
    strict_lower = (jnp.arange(tm)[:, None] > jnp.arange(tm)[None, :]).astype(BF16)
    return pl.pallas_call(
        _router_kernel,
        grid=(t // tm,),
        in_specs=[pl.BlockSpec((tm, d), lambda i: (i, 0)), _resident((1, d)), _resident((3 * d, LANES)),
                  _resident((tm, tm))],
        out_specs=[pl.BlockSpec((tm, d), lambda i: (i, 0)),
                   pl.BlockSpec((tm, LANES), lambda i: (i, 0)),
                   pl.BlockSpec((SUBLANES, LANES), lambda i: (0, 0))],
        out_shape=[jax.ShapeDtypeStruct((t, d), BF16), jax.ShapeDtypeStruct((t, LANES), F32),
                   jax.ShapeDtypeStruct((SUBLANES, LANES), F32)],
        scratch_shapes=[pltpu.VMEM((SUBLANES, LANES), F32)],
        compiler_params=_params("arbitrary"),
        name="moe_router",
    )(x, gain.reshape(1, d), wr3, strict_lower)


def _grouped_mlp_kernel(be_ref, nu_ref, xs_ref, wg_ref, wu_ref, wd_ref, o_ref, wg_s, wu_s, wd_s):
    b = pl.program_id(0)
    changed = jnp.logical_or(b == 0, be_ref[b] != be_ref[jnp.maximum(b - 1, 0)])

    @pl.when(changed)
    def _():
        wg_s[...] = wg_ref[0, 0].astype(BF16)
        wu_s[...] = wu_ref[0, 0].astype(BF16)
        wd_s[...] = wd_ref[0, 0].astype(BF16)

    @pl.when(b < nu_ref[0])
    def _():
        x = xs_ref[...]
        gate = jnp.dot(x, wg_s[...], preferred_element_type=F32)
        up = jnp.dot(x, wu_s[...], preferred_element_type=F32)
        hid = (_silu(gate) * up).astype(BF16)
        o_ref[...] = jnp.dot(hid, wd_s[...], preferred_element_type=F32).astype(o_ref.dtype)

    @pl.when(b >= nu_ref[0])
    def _():
        o_ref[...] = jnp.zeros_like(o_ref)


def _grouped_mlp(xs, block_expert, n_used, w_gate, w_up, w_down, layer, rows):
    n_slots, d = xs.shape
    ff = w_gate.shape[3]
    n_blocks = n_slots // rows
    grid_spec = pltpu.PrefetchScalarGridSpec(
        num_scalar_prefetch=2,
        grid=(n_blocks,),
        in_specs=[pl.BlockSpec((rows, d), lambda b, be, nu: (b, 0)),
                  pl.BlockSpec((1, 1, d, ff), lambda b, be, nu: (layer, be[b], 0, 0)),
                  pl.BlockSpec((1, 1, d, ff), lambda b, be, nu: (layer, be[b], 0, 0)),
                  pl.BlockSpec((1, 1, ff, d), lambda b, be, nu: (layer, be[b], 0, 0))],
        out_specs=pl.BlockSpec((rows, d), lambda b, be, nu: (b, 0)),
        scratch_shapes=[pltpu.VMEM((d, ff), BF16), pltpu.VMEM((d, ff), BF16), pltpu.VMEM((ff, d), BF16)],
    )
    return pl.pallas_call(
        _grouped_mlp_kernel,
        grid_spec=grid_spec,
        out_shape=jax.ShapeDtypeStruct((n_slots, d), BF16),
        compiler_params=_params("arbitrary"),
        name="moe_experts",
    )(block_expert, n_used, xs, w_gate, w_up, w_down)


def _combine_kernel(x_ref, y0_ref, y1_ref, route_ref, gain_ref, o_ref, *, final):
    r = route_ref[...]
    xn = x_ref[...] + r[:, 2:3] * y0_ref[...].astype(F32) + r[:, 3:4] * y1_ref[...].astype(F32)
    if final:
        ms = jnp.mean(xn * xn, axis=-1, keepdims=True)
        xn = xn * lax.rsqrt(ms + RMS_EPS) * gain_ref[...]
    o_ref[...] = xn


def _combine(x, y01, route, final_gain, tm=512):
    t, d = x.shape
    tm = min(tm, t)
    final = final_gain is not None
    gain = (final_gain if final else jnp.ones((d,), F32)).reshape(1, d)
    row = pl.BlockSpec((tm, d), lambda i: (i, 0))
    return pl.pallas_call(
        functools.partial(_combine_kernel, final=final),
        grid=(t // tm,),
        in_specs=[row, row, pl.BlockSpec((tm, d), lambda i: (i + t // tm, 0)),
                  pl.BlockSpec((tm, LANES), lambda i: (i, 0)), _resident((1, d))],
        out_specs=row,
        out_shape=jax.ShapeDtypeStruct((t, d), F32),
        compiler_params=_params("parallel"),
        name="moe_combine",
    )(x, y01, y01, route, gain)


def _moe_layer(x, p, i, final_gain=None):
    t, d = x.shape
    hb, route, cnt = _router(x, p['moe_norm'][i], p['moe_w_group'], p['moe_w_expert'], i)
    a = t * TOP_K
    r = MOE_ROWS if a >= N_EXPERTS * MOE_ROWS else MOE_ROWS_SMALL
    i32 = jnp.int32
    e_idx = route[:, :TOP_K].astype(i32)
    rank = route[:, 4:4 + TOP_K].astype(i32)
    counts = cnt[0, N_EGROUPS:N_EGROUPS + N_EXPERTS].astype(i32)
    padded = (counts + r - 1) // r * r
    pad_end = jnp.cumsum(padded)
    pad_start = pad_end - padded
    start = jnp.cumsum(counts) - counts
    dest = (pad_start[e_idx] + rank).T.reshape(a)
    n_blocks = (a + N_EXPERTS * (r - 1)) // r
    n_used = (pad_end[-1] // r).astype(i32)
    blk = jnp.arange(n_blocks, dtype=i32)
    block_expert = jnp.minimum(jnp.searchsorted(pad_end, blk * r, side='right'), N_EXPERTS - 1).astype(i32)
    order = jnp.argsort(e_idx.reshape(a), stable=True).astype(i32)
    off = blk[:, None] * r + jnp.arange(r, dtype=i32)[None, :] - pad_start[block_expert][:, None]
    live = (off < counts[block_expert][:, None]) & (blk < n_used)[:, None]
    src = jnp.clip(start[block_expert][:, None] + off, 0, a - 1)
    spread = (blk[:, None] * r + jnp.arange(r, dtype=i32)[None, :]) % t
    slot_tok = jnp.where(live, order[src.reshape(-1)].reshape(n_blocks, r) // TOP_K, spread).reshape(n_blocks * r)
    last_e = block_expert[jnp.maximum(n_used - 1, 0)]
    block_expert = jnp.where(blk < n_used, block_expert, last_e)
    xs = hb[slot_tok]
    yb = _grouped_mlp(xs, block_expert, n_used.reshape(1), p['moe_w_gate'], p['moe_w_up'], p['moe_w_down'], i, r)
    return _combine(x, yb[dest], route, final_gain)


def _attn_finish(o_halves, tq, lam, subln, out_scale):
    od = o_halves[:tq] - lam * o_halves[tq:]
    ms = jnp.mean(od * od, axis=-1, keepdims=True)
    return od * lax.rsqrt(ms + RMS_EPS) * subln * out_scale


def _attn_kernel(lam_ref, sub_ref, q_ref, k_ref, v_ref, o_ref, kb_scr, vb_scr, *, tq, tk, out_scale):
    qi = pl.program_id(2)

    @pl.when(qi == 0)
    def _():
        kb_scr[...] = k_ref[0].astype(BF16)
        vb_scr[:, :HEAD_W] = v_ref[0].astype(BF16)
        vb_scr[:, HEAD_W:] = jnp.ones((vb_scr.shape[0], HEAD_W), BF16)

    q = q_ref[0] * (ATTN_SCALE * LOG2E)
    lane = lax.broadcasted_iota(jnp.int32, (tq, HEAD_W), 1)
    qs = jnp.concatenate([jnp.where(lane < HALF_DIM, q, 0.0), jnp.where(lane >= HALF_DIM, q, 0.0)],
                         axis=0).astype(BF16)

    def scores(j):
        kblk = kb_scr[pl.ds(pl.multiple_of(j * tk, tk), tk), :]
        return lax.dot_general(qs, kblk, (((1,), (1,)), ((), ())), preferred_element_type=F32)

    def softmax_pv(j, s, m, acc, masked):
        vblk = vb_scr[pl.ds(pl.multiple_of(j * tk, tk), tk), :]
        if masked:
            rq = lax.broadcasted_iota(jnp.int32, (2 * tq, tk), 0)
            rq = jnp.where(rq >= tq, rq - tq, rq) + qi * tq
            ck = lax.broadcasted_iota(jnp.int32, (2 * tq, tk), 1) + j * tk
            s = jnp.where(ck <= rq, s, NEG_BIG)
        m_new = jnp.maximum(m, jnp.max(s, axis=-1, keepdims=True))
        pexp = jnp.exp2((s - m_new).astype(BF16))
        acc = acc * jnp.exp2(m - m_new) + jnp.dot(pexp, vblk, preferred_element_type=F32)
        return m_new, acc

    def body(j, carry):
        s, m, acc = carry
        s_next = scores(j + 1)
        m, acc = softmax_pv(j, s, m, acc, False)
        return s_next, m, acc

    init = (scores(0), jnp.full((2 * tq, 1), NEG_BIG, F32), jnp.zeros((2 * tq, 2 * HEAD_W), F32))
    n_full = (qi * tq) // tk
    s, m, acc = lax.fori_loop(0, n_full, body, init)
    m, acc = softmax_pv(n_full, s, m, acc, True)
    o = acc[:, :HEAD_W] / acc[:, HEAD_W:]
    o_ref[0] = _attn_finish(o, tq, lam_ref[...], sub_ref[...], out_scale).astype(o_ref.dtype)


def _prompt_attention(q, k, v, lam, subln, out_scale, b, s):
    d = q.shape[1]
    tq = min(ATTN_TQ, s)
    tk = min(ATTN_TK, s)
    q3, k3, v3 = (a.reshape(b, s, d) for a in (q, k, v))
    lam_row = jnp.full((1, HEAD_W), lam, F32)
    out = pl.pallas_call(
        functools.partial(_attn_kernel, tq=tq, tk=tk, out_scale=out_scale),
        grid=(b, ATTN_HEADS, s // tq),
        in_specs=[pl.BlockSpec((1, HEAD_W), lambda bi, h, qi: (0, 0)),
                  pl.BlockSpec((1, HEAD_W), lambda bi, h, qi: (0, 0)),
                  pl.BlockSpec((1, tq, HEAD_W), lambda bi, h, qi: (bi, qi, h)),
                  pl.BlockSpec((1, s, HEAD_W), lambda bi, h, qi: (bi, 0, h)),
                  pl.BlockSpec((1, s, HEAD_W), lambda bi, h, qi: (bi, 0, h))],
        out_specs=pl.BlockSpec((1, tq, HEAD_W), lambda bi, h, qi: (bi, qi, h)),
        out_shape=jax.ShapeDtypeStruct((b, s, d), BF16),
        scratch_shapes=[pltpu.VMEM((s, HEAD_W), BF16), pltpu.VMEM((s, 2 * HEAD_W), BF16)],
        compiler_params=_params("parallel", "parallel", "arbitrary"),
        name="prompt_attention",
    )(lam_row, subln.reshape(1, HEAD_W), q3, k3, v3)
    return out.reshape(b * s, d)


def _paged_kernel(pt_ref, lam_ref, sub_ref, qm_ref, kn_ref, vn_ref, *rest, n_pg, tqn, out_scale):
    k_refs = rest[:n_pg]
    v_refs = rest[n_pg:2 * n_pg]
    o_ref = rest[2 * n_pg]
    m_scr, l_scr, acc_scr = rest[2 * n_pg + 1:]
    si = pl.program_id(1)
    ns = pl.num_programs(1)
    rows = ATTN_HEADS * 2 * tqn
    cols = PAGE_SIZE * ATTN_HEADS

    @pl.when(si == 0)
    def _():
        m_scr[...] = jnp.full_like(m_scr, NEG_BIG)
        l_scr[...] = jnp.zeros_like(l_scr)
        acc_scr[...] = jnp.zeros_like(acc_scr)

    qm = qm_ref[0].astype(BF16)
    nt = (((1,), (1,)), ((), ()))
    same_head = (lax.broadcasted_iota(jnp.int32, (rows, cols), 0) // (2 * tqn)
                 == lax.broadcasted_iota(jnp.int32, (rows, cols), 1) % ATTN_HEADS)
    scores = []
    for g in range(n_pg):
        k2 = k_refs[g][0].reshape(cols, HEAD_W).astype(BF16)
        s = lax.dot_general(qm, k2, nt, preferred_element_type=F32)
        scores.append(jnp.where(same_head, s, NEG_BIG))
    m_old = m_scr[...]
    m_new = m_old
    for s in scores:
        m_new = jnp.maximum(m_new, jnp.max(s, axis=-1, keepdims=True))
    corr = jnp.exp(m_old - m_new)
    l = l_scr[...] * corr
    acc = acc_scr[...] * corr
    for g in range(n_pg):
        pexp = jnp.exp(scores[g] - m_new)
        l = l + jnp.sum(pexp, axis=-1, keepdims=True)
        v2 = v_refs[g][0].reshape(cols, HEAD_W).astype(BF16)
        acc = acc + jnp.dot(pexp.astype(BF16), v2, preferred_element_type=F32)
    m_scr[...] = m_new
    l_scr[...] = l
    acc_scr[...] = acc

    @pl.when(si == ns - 1)
    def _():
        npad = kn_ref.shape[1]
        s = lax.dot_general(qm, kn_ref[0].astype(BF16), nt, preferred_element_type=F32)
        r = lax.broadcasted_iota(jnp.int32, (rows, npad), 0)
        c = lax.broadcasted_iota(jnp.int32, (rows, npad), 1)
        ok = (c < tqn * ATTN_HEADS) & (c % ATTN_HEADS == r // (2 * tqn)) & (c // ATTN_HEADS <= r % tqn)
        s = jnp.where(ok, s, NEG_BIG)
        m1 = jnp.maximum(m_new, jnp.max(s, axis=-1, keepdims=True))
        pexp = jnp.exp(s - m1)
        corr1 = jnp.exp(m_new - m1)
        l1 = l * corr1 + jnp.sum(pexp, axis=-1, keepdims=True)
        acc1 = acc * corr1 + jnp.dot(pexp.astype(BF16), vn_ref[0].astype(BF16), preferred_element_type=F32)
        o = acc1 / l1
        od = o - lam_ref[...] * pltpu.roll(o, rows - tqn, 0)
        ms = jnp.mean(od * od, axis=-1, keepdims=True)
        o_ref[0] = od * lax.rsqrt(ms + RMS_EPS) * sub_ref[...] * out_scale


def _paged_attention(q, k_new, v_new, cache_k, cache_v, page_table, lam, subln, out_scale):
    b, n_pages = page_table.shape
    tqn = q.shape[0] // b
    n_pg = PAGES_PER_STEP
    rows = ATTN_HEADS * 2 * tqn
    q4 = q.reshape(b, tqn, ATTN_HEADS, HEAD_W).transpose(0, 2, 1, 3) * ATTN_SCALE
    lane = jnp.arange(HEAD_W)
    qm = jnp.stack([jnp.where(lane < HALF_DIM, q4, 0.0), jnp.where(lane >= HALF_DIM, q4, 0.0)], axis=2)
    qm = qm.reshape(b, rows, HEAD_W)
    npad = LANES
    kn = jnp.pad(k_new.reshape(b, tqn * ATTN_HEADS, HEAD_W), ((0, 0), (0, npad - tqn * ATTN_HEADS), (0, 0)))
    vn = jnp.pad(v_new.reshape(b, tqn * ATTN_HEADS, HEAD_W), ((0, 0), (0, npad - tqn * ATTN_HEADS), (0, 0)))
    lam_row = jnp.full((1, HEAD_W), lam, F32)

    def page_spec(g):
        return pl.BlockSpec((1, PAGE_SIZE, ATTN_HEADS, HEAD_W),
                            lambda bi, si, pt: (pt[bi, si * n_pg + g], 0, 0, 0))

    per_seq = lambda r: pl.BlockSpec((1, r, HEAD_W), lambda bi, si, pt: (bi, 0, 0))
    grid_spec = pltpu.PrefetchScalarGridSpec(
        num_scalar_prefetch=1,
        grid=(b, n_pages // n_pg),
        in_specs=[pl.BlockSpec((1, HEAD_W), lambda bi, si, pt: (0, 0)),
                  pl.BlockSpec((1, HEAD_W), lambda bi, si, pt: (0, 0)),
                  per_seq(rows), per_seq(npad), per_seq(npad)]
                 + [page_spec(g) for g in range(n_pg)] + [page_spec(g) for g in range(n_pg)],
        out_specs=per_seq(rows),
        scratch_shapes=[pltpu.VMEM((rows, 1), F32), pltpu.VMEM((rows, 1), F32), pltpu.VMEM((rows, HEAD_W), F32)],
    )
    o = pl.pallas_call(
        functools.partial(_paged_kernel, n_pg=n_pg, tqn=tqn, out_scale=out_scale),
        grid_spec=grid_spec,
        out_shape=jax.ShapeDtypeStruct((b, rows, HEAD_W), F32),
        compiler_params=_params("parallel", "arbitrary"),
        name="paged_attention",
    )(page_table, lam_row, subln.reshape(1, HEAD_W), qm, kn, vn, *([cache_k] * n_pg), *([cache_v] * n_pg))
    o = o.reshape(b, ATTN_HEADS, 2, tqn, HEAD_W)[:, :, 0]
    return o.transpose(0, 2, 1, 3).reshape(b * tqn, ATTN_HEADS * HEAD_W).astype(BF16)


def _trunk(x, conv_buf, ssm_state, p, attention):
    b, l, d = x.shape
    t = b * l
    x, new_conv, new_state = _mamba_layer(x, conv_buf, ssm_state, p, 0)
    x = _moe_layer(x.reshape(t, d), p, 0)
    k, v = _norm_matmul(x, p['kv_norm'], [p['w_k'].astype(BF16), p['w_v'].astype(BF16)])
    layer = N_A_LAYERS
    lam_init = 0.8 - 0.6 * math.exp(-0.3 * layer)
    lq = p['attn_lambda'][0]
    lam = jnp.exp(jnp.sum(lq[0] * lq[1])) - jnp.exp(jnp.sum(lq[2] * lq[3])) + lam_init
    q, = _norm_matmul(x, p['attn_norm'][0], [p['attn_w_q'][0].astype(BF16)])
    o = attention(q, k, v, lam, p['attn_subln'][0], 1.0 - lam_init)
    x = _matmul_res(o, p['attn_w_o'][0].astype(BF16), x)
    y = _moe_layer(x, p, layer, final_gain=p['final_norm'])
    kv_shape = (b, l, ATTN_HEADS, HEAD_W)
    return y.reshape(b, l, d), new_conv[None], new_state[None], k.reshape(kv_shape), v.reshape(kv_shape)


def kernel(x_prompt, x_sample, state_ssm, state_conv, cache_k, cache_v, page_table, ssm_norm, ssm_w_in, ssm_conv_w, ssm_conv_b, ssm_dt_bias, ssm_a_log, ssm_d, ssm_gate_norm, ssm_w_out, kv_norm, w_k, w_v, attn_norm, attn_w_q, attn_lambda, attn_subln, attn_w_o, moe_norm, moe_w_group, moe_w_expert, moe_w_gate, moe_w_up, moe_w_down, final_norm):
    p = dict(ssm_norm=ssm_norm, ssm_w_in=ssm_w_in, ssm_conv_w=ssm_conv_w, ssm_conv_b=ssm_conv_b,
             ssm_dt_bias=ssm_dt_bias, ssm_a_log=ssm_a_log, ssm_d=ssm_d, ssm_gate_norm=ssm_gate_norm,
             ssm_w_out=ssm_w_out, kv_norm=kv_norm, w_k=w_k, w_v=w_v, attn_norm=attn_norm,
             attn_w_q=attn_w_q, attn_lambda=attn_lambda, attn_subln=attn_subln, attn_w_o=attn_w_o,
             moe_norm=moe_norm, moe_w_group=moe_w_group, moe_w_expert=moe_w_expert,
             moe_w_gate=moe_w_gate, moe_w_up=moe_w_up, moe_w_down=moe_w_down, final_norm=final_norm)
    bp, sp, _ = x_prompt.shape
    conv0 = jnp.zeros((bp, CONV_W - 1, CONV_DIM), F32)
    ssm0 = jnp.zeros((bp, SSM_HEADS, SSM_HEAD_DIM, D_STATE), F32)
    prompt_attn = functools.partial(_prompt_attention, b=bp, s=sp)
    y_p, conv_p, ssm_p, k_p, v_p = _trunk(x_prompt, conv0, ssm0, p, prompt_attn)

    def sample_attn(q, k, v, lam, subln, out_scale):
        return _paged_attention(q, k, v, cache_k, cache_v, page_table, lam, subln, out_scale)

    y_s, conv_s, ssm_s, k_s, v_s = _trunk(x_sample, state_conv[0], state_ssm[0], p, sample_attn)
    return (y_p, y_s, ssm_p, conv_p, k_p, v_p, ssm_s, conv_s, k_s, v_s)
```

```python
import functools
import math

import jax
import jax.numpy as jnp
from jax import lax
from jax.experimental import pallas as pl
from jax.experimental.pallas import tpu as pltpu

F32 = jnp.float32
BF16 = jnp.bfloat16

D_MODEL = 1024
DEPTH = 2
N_A_LAYERS = DEPTH // 2
D_INNER = 2 * D_MODEL
SSM_HEAD_DIM = 64
SSM_HEADS = D_INNER // SSM_HEAD_DIM
SSM_GROUPS = 4
HEADS_PER_GROUP = SSM_HEADS // SSM_GROUPS
D_STATE = 128
CONV_W = 4
CONV_DIM = D_INNER + 2 * SSM_GROUPS * D_STATE
ZX_DIM = D_INNER + CONV_DIM
SSD_CHUNK = 128
GATED_NORM_EPS = 1e-5
HALF_DIM = 64
ATTN_HEADS = D_MODEL // (2 * HALF_DIM)
HEAD_W = 2 * HALF_DIM
ATTN_SCALE = HALF_DIM ** -0.5
N_EGROUPS = 4
EXPERTS_PER_GROUP = 8
N_EXPERTS = N_EGROUPS * EXPERTS_PER_GROUP
TOP_K = 2
EXPERT_FF = D_MODEL // 2
RMS_EPS = 1e-6
PAGE_SIZE = 128

LANES = 128
SUBLANES = 8
MOE_ROWS = 512
MOE_ROWS_SMALL = 128
ATTN_TQ = 512
ATTN_TK = 512
PAGES_PER_STEP = 16
COL_CHUNK = 1024
LOG2E = 1.4426950408889634
NEG_BIG = -1e30
VMEM_LIMIT = 56 * 1024 * 1024


def _params(*sem):
    return pltpu.CompilerParams(dimension_semantics=sem, vmem_limit_bytes=VMEM_LIMIT)


def _silu(x):
    return x * jax.nn.sigmoid(x)


def _split3(x):
    hi = x.astype(BF16)
    r1 = x - hi.astype(F32)
    mid = r1.astype(BF16)
    lo = (r1 - mid.astype(F32)).astype(BF16)
    return jnp.concatenate([hi, mid, lo], axis=1)


def _norm_matmul_kernel(*refs, n_w, has_small, per_head):
    x_ref, g_ref = refs[0], refs[1]
    w_refs = refs[2:2 + n_w]
    pos = 2 + n_w
    ws_ref = refs[pos] if has_small else None
    pos += int(has_small)
    o_refs = refs[pos:pos + n_w]
    pos += n_w
    os_ref = refs[pos] if has_small else None
    pos += int(has_small)
    ob_refs = refs[pos:pos + n_w] if per_head else None

    x = x_ref[...]
    ms = jnp.mean(x * x, axis=-1, keepdims=True)
    h = (x * lax.rsqrt(ms + RMS_EPS) * g_ref[...]).astype(BF16)
    if has_small:
        os_ref[...] = jnp.dot(h, ws_ref[...], preferred_element_type=F32)
    for k, (w_ref, o_ref) in enumerate(zip(w_refs, o_refs)):
        n = w_ref.shape[1]
        step = min(COL_CHUNK, n)
        for c in range(0, n, step):
            res = jnp.dot(h, w_ref[:, c:c + step], preferred_element_type=F32)
            if per_head:
                for hh in range(step // HEAD_W):
                    o_ref[:, c // HEAD_W + hh, :] = res[:, hh * HEAD_W:(hh + 1) * HEAD_W]
                ob_refs[k][:, c:c + step] = res.astype(BF16)
            else:
                o_ref[:, c:c + step] = res


def _resident(shape):
    return pl.BlockSpec(shape, lambda i: (0,) * len(shape), pipeline_mode=pl.Buffered(1))


def _norm_matmul(x, gain, weights, w_small=None, per_head=False, tm=512):
    t, d = x.shape
    n = weights[0].shape[1]
    tm = min(tm, t)
    n_w = len(weights)
    has_small = w_small is not None
    in_specs = [pl.BlockSpec((tm, d), lambda i: (i, 0)), _resident((1, d))]
    in_specs += [_resident((d, n))] * n_w
    if per_head:
        out_shape = [jax.ShapeDtypeStruct((t, n // HEAD_W, HEAD_W), F32)] * n_w
        out_specs = [pl.BlockSpec((tm, n // HEAD_W, HEAD_W), lambda i: (i, 0, 0))] * n_w
    else:
        out_shape = [jax.ShapeDtypeStruct((t, n), F32)] * n_w
        out_specs = [pl.BlockSpec((tm, n), lambda i: (i, 0))] * n_w
    args = [x, gain.reshape(1, d)] + list(weights)
    if has_small:
        in_specs.append(_resident((d, LANES)))
        out_shape.append(jax.ShapeDtypeStruct((t, LANES), F32))
        out_specs.append(pl.BlockSpec((tm, LANES), lambda i: (i, 0)))
        args.append(w_small)
    if per_head:
        out_shape += [jax.ShapeDtypeStruct((t, n), BF16)] * n_w
        out_specs += [pl.BlockSpec((tm, n), lambda i: (i, 0))] * n_w
    return pl.pallas_call(
        functools.partial(_norm_matmul_kernel, n_w=n_w, has_small=has_small, per_head=per_head),
        grid=(t // tm,),
        in_specs=in_specs, out_specs=out_specs, out_shape=out_shape,
        compiler_params=_params("parallel"),
        name="norm_matmul",
    )(*args)


def _matmul_res_kernel(a_ref, w_ref, r_ref, o_ref):
    o_ref[...] = r_ref[...] + jnp.dot(a_ref[...], w_ref[...], preferred_element_type=F32)


def _matmul_res(a, w, res, tm=512):
    t, k = a.shape
    n = w.shape[1]
    tm = min(tm, t)
    return pl.pallas_call(
        _matmul_res_kernel,
        grid=(t // tm,),
        in_specs=[pl.BlockSpec((tm, k), lambda i: (i, 0)), _resident((k, n)),
                  pl.BlockSpec((tm, n), lambda i: (i, 0))],
        out_specs=pl.BlockSpec((tm, n), lambda i: (i, 0)),
        out_shape=jax.ShapeDtypeStruct((t, n), F32),
        compiler_params=_params("parallel"),
        name="matmul_res",
    )(a, w, res)


def _ssd_kernel(zx_ref, dt_ref, cbuf_ref, h0_ref, convw_ref, convb_ref, dtb_ref, alog_ref, dfull_ref,
                gnorm_ref, tri_ref, e_ref, e2_ref, g_ref, ht_ref, state_scr, tail_scr, *, tc, valid):
    ci = pl.program_id(1)
    nc = pl.num_programs(1)
    gw = HEADS_PER_GROUP * SSM_HEAD_DIM
    n_pairs = HEADS_PER_GROUP // 2

    @pl.when(ci == 0)
    def _init():
        tail_scr[...] = cbuf_ref[0]
        for g in range(SSM_GROUPS):
            for j in range(n_pairs):
                h2 = h0_ref[0, pl.ds(g * HEADS_PER_GROUP + 2 * j, 2)]
                state_scr[g, :, j * LANES:(j + 1) * LANES] = h2.reshape(2 * SSM_HEAD_DIM, D_STATE).T

    xr = zx_ref[0, :, D_INNER:ZX_DIM]
    xcat = jnp.concatenate([tail_scr[...], xr], axis=0)
    conv = convb_ref[...] + convw_ref[CONV_W - 1:CONV_W, :] * xr
    for k in range(1, CONV_W):
        conv = conv + convw_ref[CONV_W - 1 - k:CONV_W - k, :] * xcat[SUBLANES - k:SUBLANES - k + tc]
    tail_scr[...] = xr[tc - SUBLANES:tc]
    xbc = _silu(conv)
    xs = xbc[:, :D_INNER]
    bm = xbc[:, D_INNER:D_INNER + SSM_GROUPS * D_STATE]
    cm = xbc[:, D_INNER + SSM_GROUPS * D_STATE:]

    dtl = dt_ref[0] + dtb_ref[...]
    dt = jnp.maximum(dtl, 0.0) + jnp.log1p(jnp.exp(-jnp.abs(dtl)))
    row = lax.broadcasted_iota(jnp.int32, (tc, LANES), 0)
    if valid is not None:
        dt = jnp.where(row < valid, dt, 0.0)
    la = dt * (-jnp.exp(alog_ref[...]))
    cs3 = jnp.dot(tri_ref[...], _split3(la), preferred_element_type=F32)
    cs = cs3[:, :LANES] + cs3[:, LANES:2 * LANES] + cs3[:, 2 * LANES:]
    cs_last = cs[tc - 1:tc, :]
    ecs = jnp.exp(cs)
    to_end = jnp.exp(cs_last - cs)
    cdecay = jnp.exp(jnp.broadcast_to(cs_last, (SUBLANES, LANES)))

    lane_id = lax.broadcasted_iota(jnp.int32, (tc, LANES), 1)

    def expand(v, e):
        hi = v.astype(BF16).astype(F32)
        r1 = v - hi
        mid = r1.astype(BF16).astype(F32)
        lo = r1 - mid
        lanes = lane_id[:v.shape[0]]
        packed = jnp.where(lanes < SSM_HEADS, hi,
                           jnp.where(lanes < 2 * SSM_HEADS, pltpu.roll(mid, SSM_HEADS, 1),
                                     pltpu.roll(lo, 2 * SSM_HEADS, 1)))
        return jnp.dot(packed.astype(BF16), e, preferred_element_type=F32)

    e_mat = e_ref[...]
    dt_full = expand(dt, e_mat)
    ecs_full = expand(ecs, e_mat)
    toend_full = expand(to_end, e_mat)
    cdecay_full = expand(cdecay, e_mat)[0:1, :]
    cs_col = expand(cs, e2_ref[...])
    cs_row = cs.T

    xdt = xs * dt_full
    xdt_b = xdt.astype(BF16)
    xw_b = (xdt * toend_full).astype(BF16)
    z = zx_ref[0, :, :D_INNER]

    tri_mask = lax.broadcasted_iota(jnp.int32, (tc, tc), 0) >= lax.broadcasted_iota(jnp.int32, (tc, tc), 1)
    lane_lo = lax.broadcasted_iota(jnp.int32, (tc, LANES), 1) < SSM_HEAD_DIM

    for g in range(SSM_GROUPS):
        gs = slice(g * gw, (g + 1) * gw)
        b_g = bm[:, g * D_STATE:(g + 1) * D_STATE]
        c_g = cm[:, g * D_STATE:(g + 1) * D_STATE].astype(BF16)
        cb = lax.dot_general(c_g, b_g.astype(BF16), (((1,), (1,)), ((), ())), preferred_element_type=F32)
        bt_g = b_g.T.astype(BF16)
        st_old = state_scr[g]
        y_off = jnp.dot(c_g, st_old.astype(BF16), preferred_element_type=F32)
        s_new = jnp.dot(bt_g, xw_b[:, gs], preferred_element_type=F32)
        state_scr[g] = st_old * cdecay_full[:, gs] + s_new
        pairs = []
        for j in range(n_pairs):
            x_pair = xdt_b[:, g * gw + j * LANES:g * gw + (j + 1) * LANES]
            acc = None
            for r2 in range(2):
                h = g * HEADS_PER_GROUP + 2 * j + r2
                seg = cs_col[:, h * LANES:(h + 1) * LANES] - cs_row[h:h + 1, :]
                m = (jnp.exp(jnp.where(tri_mask, seg, NEG_BIG)) * cb).astype(BF16)
                keep = lane_lo if r2 == 0 else jnp.logical_not(lane_lo)
                rhs = jnp.where(keep, x_pair, jnp.zeros_like(x_pair))
                part = jnp.dot(m, rhs, preferred_element_type=F32)
                acc = part if acc is None else acc + part
            pairs.append(acc)
        y = jnp.concatenate(pairs, axis=1) + y_off * ecs_full[:, gs] + xs[:, gs] * dfull_ref[:, gs]
        gated = y * _silu(z[:, gs])
        ms = jnp.mean(gated * gated, axis=-1, keepdims=True)
        g_ref[0, :, gs] = (gated * lax.rsqrt(ms + GATED_NORM_EPS) * gnorm_ref[:, gs]).astype(g_ref.dtype)

    @pl.when(ci == nc - 1)
    def _fin():
        for g in range(SSM_GROUPS):
            for j in range(n_pairs):
                st = state_scr[g, :, j * LANES:(j + 1) * LANES].T
                ht_ref[0, pl.ds(g * HEADS_PER_GROUP + 2 * j, 2)] = st.reshape(2, SSM_HEAD_DIM, D_STATE)


def _ssd_constants(tc):
    tri = (jnp.arange(tc)[:, None] >= jnp.arange(tc)[None, :]).astype(BF16)
    head = jnp.arange(LANES)

    def expansion(width):
        cols = jnp.arange(SSM_HEADS * width) // width
        piece = head % SSM_HEADS
        return ((piece[:, None] == cols[None, :]) & (head[:, None] < 3 * SSM_HEADS)).astype(BF16)

    return tri, expansion(SSM_HEAD_DIM), expansion(LANES)


def _ssd(zx, dtr, cbuf8, h0, conv_w, conv_b, dt_bias, a_log, d_skip, gate_norm, valid):
    b, lp, _ = zx.shape
    tc = SSD_CHUNK
    nc = lp // tc
    tri, e_mat, e2_mat = _ssd_constants(tc)
    pad = LANES - SSM_HEADS
    dtb = jnp.pad(dt_bias, (0, pad)).reshape(1, LANES)
    alog = jnp.pad(a_log, (0, pad)).reshape(1, LANES)
    dfull = jnp.repeat(d_skip, SSM_HEAD_DIM).reshape(1, D_INNER)
    const = lambda *shape: pl.BlockSpec(shape, lambda bi, ci: (0,) * len(shape))
    return pl.pallas_call(
        functools.partial(_ssd_kernel, tc=tc, valid=valid),
        grid=(b, nc),
        in_specs=[pl.BlockSpec((1, tc, ZX_DIM), lambda bi, ci: (bi, ci, 0)),
                  pl.BlockSpec((1, tc, LANES), lambda bi, ci: (bi, ci, 0)),
                  pl.BlockSpec((1, SUBLANES, CONV_DIM), lambda bi, ci: (bi, 0, 0)),
                  pl.BlockSpec((1, SSM_HEADS, SSM_HEAD_DIM, D_STATE), lambda bi, ci: (bi, 0, 0, 0)),
                  const(CONV_W, CONV_DIM), const(1, CONV_DIM), const(1, LANES), const(1, LANES),
                  const(1, D_INNER), const(1, D_INNER),
                  const(tc, tc), const(LANES, D_INNER), const(LANES, SSM_HEADS * LANES)],
        out_specs=[pl.BlockSpec((1, tc, D_INNER), lambda bi, ci: (bi, ci, 0)),
                   pl.BlockSpec((1, SSM_HEADS, SSM_HEAD_DIM, D_STATE), lambda bi, ci: (bi, 0, 0, 0))],
        out_shape=[jax.ShapeDtypeStruct((b, lp, D_INNER), BF16),
                   jax.ShapeDtypeStruct((b, SSM_HEADS, SSM_HEAD_DIM, D_STATE), F32)],
        scratch_shapes=[pltpu.VMEM((SSM_GROUPS, D_STATE, HEADS_PER_GROUP * SSM_HEAD_DIM), F32),
                        pltpu.VMEM((SUBLANES, CONV_DIM), F32)],
        compiler_params=_params("parallel", "arbitrary"),
        name="ssd",
    )(zx, dtr, cbuf8, h0, conv_w, conv_b.reshape(1, CONV_DIM), dtb, alog, dfull,
      gate_norm.reshape(1, D_INNER), tri, e_mat, e2_mat)


def _mamba_layer(x, conv_buf, ssm_state, p, j):
    b, l, d = x.shape
    t = b * l
    w_in = p['ssm_w_in'][j]
    w_zx = w_in[:, :ZX_DIM].astype(BF16)
    w_dt = jnp.pad(w_in[:, ZX_DIM:], ((0, 0), (0, LANES - SSM_HEADS))).astype(BF16)
    zx, dtr = _norm_matmul(x.reshape(t, d), p['ssm_norm'][j], [w_zx], w_small=w_dt)
    zx = zx.reshape(b, l, ZX_DIM)
    dtr = dtr.reshape(b, l, LANES)
    xbc_raw = zx[:, :, D_INNER:]
    new_conv = jnp.concatenate([conv_buf, xbc_raw[:, max(l - (CONV_W - 1), 0):]], axis=1)[:, -(CONV_W - 1):]
    lp = -(-l // SSD_CHUNK) * SSD_CHUNK
    valid = None
    if lp != l:
        valid = l
        zx = jnp.pad(zx, ((0, 0), (0, lp - l), (0, 0)))
        dtr = jnp.pad(dtr, ((0, 0), (0, lp - l), (0, 0)))
    cbuf8 = jnp.pad(conv_buf, ((0, 0), (SUBLANES - (CONV_W - 1), 0), (0, 0)))
    gated, new_state = _ssd(zx, dtr, cbuf8, ssm_state, p['ssm_conv_w'][j], p['ssm_conv_b'][j],
                            p['ssm_dt_bias'][j], p['ssm_a_log'][j], p['ssm_d'][j], p['ssm_gate_norm'][j], valid)
    gated = gated[:, :l].reshape(t, D_INNER)
    x_new = _matmul_res(gated, p['ssm_w_out'][j].astype(BF16), x.reshape(t, d))
    return x_new.reshape(b, l, d), new_conv, new_state


def _router_kernel(x_ref, g_ref, wr_ref, low_ref, h_ref, route_ref, cnt_ref, route_t_ref, carry_scr):
    @pl.when(pl.program_id(0) == 0)
    def _():
        carry_scr[...] = jnp.zeros_like(carry_scr)

    x = x_ref[...]
    ms = jnp.mean(x * x, axis=-1, keepdims=True)
    h = x * lax.rsqrt(ms + RMS_EPS) * g_ref[...]
    h1 = h.astype(BF16)
    h_ref[...] = h1
    h2 = (h - h1.astype(F32)).astype(BF16)
    logits = jnp.dot(jnp.concatenate([h1, h1, h2], axis=1), wr_ref[...], preferred_element_type=F32)
    tm = x.shape[0]
    lane = lax.broadcasted_iota(jnp.int32, (tm, LANES), 1)
    gl = jnp.where(lane < N_EGROUPS, logits, NEG_BIG)
    gmax = jnp.max(gl, axis=-1, keepdims=True)
    g_sel = jnp.min(jnp.where(gl == gmax, lane, LANES), axis=-1, keepdims=True)
    p_group = 1.0 / jnp.sum(jnp.exp(gl - gmax), axis=-1, keepdims=True)
    lo = N_EGROUPS + EXPERTS_PER_GROUP * g_sel
    el = jnp.where((lane >= lo) & (lane < lo + EXPERTS_PER_GROUP), logits, NEG_BIG)
    v1 = jnp.max(el, axis=-1, keepdims=True)
    i1 = jnp.min(jnp.where(el == v1, lane, LANES), axis=-1, keepdims=True)
    el2 = jnp.where(lane == i1, NEG_BIG, el)
    v2 = jnp.max(el2, axis=-1, keepdims=True)
    i2 = jnp.min(jnp.where(el2 == v2, lane, LANES), axis=-1, keepdims=True)
    e2 = jnp.exp(v2 - v1)
    gate1 = p_group / (1.0 + e2)
    gate2 = p_group * e2 / (1.0 + e2)
    hit1 = lane == i1
    hit2 = lane == i2
    sel = jnp.where(hit1 | hit2, 1.0, 0.0)
    before = jnp.dot(low_ref[...], sel.astype(BF16), preferred_element_type=F32) + carry_scr[0:1, :]
    rank1 = jnp.sum(jnp.where(hit1, before, 0.0), axis=-1, keepdims=True)
    rank2 = jnp.sum(jnp.where(hit2, before, 0.0), axis=-1, keepdims=True)
    carry_scr[...] = carry_scr[...] + jnp.sum(sel, axis=0, keepdims=True)
    cnt_ref[...] = carry_scr[...]
    vals = ((i1 - N_EGROUPS).astype(F32), (i2 - N_EGROUPS).astype(F32), gate1, gate2, rank1, rank2)
    out = jnp.zeros((tm, LANES), F32)
    for k, v in enumerate(vals):
        out = jnp.where(lane == k, v, out)
    route_ref[...] = out
    route_t_ref[...] = out.T[:SUBLANES, :]


def _router(x, gain, w_group, w_expert, layer, tm=512):
    t, d = x.shape
    tm = min(tm, t)
    wr = jnp.concatenate([w_group, w_expert], axis=2)
    wr = jnp.pad(wr, ((0, 0), (0, 0), (0, LANES - wr.shape[2])))
    w1 = wr.astype(BF16)
    w2 = (wr - w1.astype(F32)).astype(BF16)
    wr3 = jnp.concatenate([w1, w2, w1], axis=1)[layer]
    strict_lower = (jnp.arange(tm)[:, None] > jnp.arange(tm)[None, :]).astype(BF16)
    return pl.pallas_call(
        _router_kernel,
        grid=(t // tm,),
        in_specs=[pl.BlockSpec((tm, d), lambda i: (i, 0)), _resident((1, d)), _resident((3 * d, LANES)),
                  _resident((tm, tm))],
        out_specs=[pl.BlockSpec((tm, d), lambda i: (i, 0)),
                   pl.BlockSpec((tm, LANES), lambda i: (i, 0)),
                   pl.BlockSpec((SUBLANES, LANES), lambda i: (0, 0)),
                   pl.BlockSpec((SUBLANES, tm), lambda i: (0, i))],
        out_shape=[jax.ShapeDtypeStruct((t, d), BF16), jax.ShapeDtypeStruct((t, LANES), F32),
                   jax.ShapeDtypeStruct((SUBLANES, LANES), F32), jax.ShapeDtypeStruct((SUBLANES, t), F32)],
        scratch_shapes=[pltpu.VMEM((SUBLANES, LANES), F32)],
        compiler_params=_params("arbitrary"),
        name="moe_router",
    )(x, gain.reshape(1, d), wr3, strict_lower)


def _grouped_mlp_kernel(be_ref, nu_ref, xs_ref, wg_ref, wu_ref, wd_ref, o_ref, wg_s, wu_s, wd_s):
    b = pl.program_id(0)
    changed = jnp.logical_or(b == 0, be_ref[b] != be_ref[jnp.maximum(b - 1, 0)])

    @pl.when(changed)
    def _():
        wg_s[...] = wg_ref[0, 0].astype(BF16)
        wu_s[...] = wu_ref[0, 0].astype(BF16)
        wd_s[...] = wd_ref[0, 0].astype(BF16)

    @pl.when(b < nu_ref[0])
    def _():
        x = xs_ref[...]
        gate = jnp.dot(x, wg_s[...], preferred_element_type=F32)
        up = jnp.dot(x, wu_s[...], preferred_element_type=F32)
        hid = (_silu(gate) * up).astype(BF16)
        o_ref[...] = jnp.dot(hid, wd_s[...], preferred_element_type=F32).astype(o_ref.dtype)

    @pl.when(b >= nu_ref[0])
    def _():
        o_ref[...] = jnp.zeros_like(o_ref)


def _grouped_mlp(xs, block_expert, n_used, w_gate, w_up, w_down, layer, rows):
    n_slots, d = xs.shape
    ff = w_gate.shape[3]
    n_blocks = n_slots // rows
    grid_spec = pltpu.PrefetchScalarGridSpec(
        num_scalar_prefetch=2,
        grid=(n_blocks,),
        in_specs=[pl.BlockSpec((rows, d), lambda b, be, nu: (b, 0)),
                  pl.BlockSpec((1, 1, d, ff), lambda b, be, nu: (layer, be[b], 0, 0)),
                  pl.BlockSpec((1, 1, d, ff), lambda b, be, nu: (layer, be[b], 0, 0)),
                  pl.BlockSpec((1, 1, ff, d), lambda b, be, nu: (layer, be[b], 0, 0))],
        out_specs=pl.BlockSpec((rows, d), lambda b, be, nu: (b, 0)),
        scratch_shapes=[pltpu.VMEM((d, ff), BF16), pltpu.VMEM((d, ff), BF16), pltpu.VMEM((ff, d), BF16)],
    )
    return pl.pallas_call(
        _grouped_mlp_kernel,
        grid_spec=grid_spec,
        out_shape=jax.ShapeDtypeStruct((n_slots, d), BF16),
        compiler_params=_params("arbitrary"),
        name="moe_experts",
    )(block_expert, n_used, xs, w_gate, w_up, w_down)


def _combine_kernel(x_ref, y0_ref, y1_ref, route_ref, gain_ref, o_ref, *, final):
    r = route_ref[...]
    xn = x_ref[...] + r[:, 2:3] * y0_ref[...].astype(F32) + r[:, 3:4] * y1_ref[...].astype(F32)
    if final:
        ms = jnp.mean(xn * xn, axis=-1, keepdims=True)
        xn = xn * lax.rsqrt(ms + RMS_EPS) * gain_ref[...]
    o_ref[...] = xn


def _combine(x, y01, route, final_gain, tm=512):
    t, d = x.shape
    tm = min(tm, t)
    final = final_gain is not None
    gain = (final_gain if final else jnp.ones((d,), F32)).reshape(1, d)
    row = pl.BlockSpec((tm, d), lambda i: (i, 0))
    return pl.pallas_call(
        functools.partial(_combine_kernel, final=final),
        grid=(t // tm,),
        in_specs=[row, row, pl.BlockSpec((tm, d), lambda i: (i + t // tm, 0)),
                  pl.BlockSpec((tm, LANES), lambda i: (i, 0)), _resident((1, d))],
        out_specs=row,
        out_shape=jax.ShapeDtypeStruct((t, d), F32),
        compiler_params=_params("parallel"),
        name="moe_combine",
    )(x, y01, y01, route, gain)


def _moe_layer(x, p, i, final_gain=None):
    t, d = x.shape
    hb, route, cnt, route_t = _router(x, p['moe_norm'][i], p['moe_w_group'], p['moe_w_expert'], i)
    a = t * TOP_K
    r = MOE_ROWS if a >= N_EXPERTS * MOE_ROWS else MOE_ROWS_SMALL
    i32 = jnp.int32
    e_idx = route_t[:TOP_K].astype(i32)
    rank = route_t[4:4 + TOP_K].astype(i32)
    counts = cnt[0, N_EGROUPS:N_EGROUPS + N_EXPERTS].astype(i32)
    padded = (counts + r - 1) // r * r
    pad_end = jnp.cumsum(padded)
    pad_start = pad_end - padded
    start = jnp.cumsum(counts) - counts
    dest = (pad_start[e_idx] + rank).reshape(a)
    n_blocks = (a + N_EXPERTS * (r - 1)) // r
    n_used = (pad_end[-1] // r).astype(i32)
    blk = jnp.arange(n_blocks, dtype=i32)
    block_expert = jnp.minimum(jnp.searchsorted(pad_end, blk * r, side='right'), N_EXPERTS - 1).astype(i32)
    order = jnp.argsort(e_idx.T.reshape(a), stable=True).astype(i32)
    off = blk[:, None] * r + jnp.arange(r, dtype=i32)[None, :] - pad_start[block_expert][:, None]
    live = (off < counts[block_expert][:, None]) & (blk < n_used)[:, None]
    src = jnp.clip(start[block_expert][:, None] + off, 0, a - 1)
    spread = (blk[:, None] * r + jnp.arange(r, dtype=i32)[None, :]) % t
    slot_tok = jnp.where(live, order[src.reshape(-1)].reshape(n_blocks, r) // TOP_K, spread).reshape(n_blocks * r)
    last_e = block_expert[jnp.maximum(n_used - 1, 0)]
    block_expert = jnp.where(blk < n_used, block_expert, last_e)
    xs = hb[slot_tok]
    yb = _grouped_mlp(xs, block_expert, n_used.reshape(1), p['moe_w_gate'], p['moe_w_up'], p['moe_w_down'], i, r)
    return _combine(x, yb[dest], route, final_gain)


def _attn_finish(o_halves, tq, lam, subln, out_scale):
    od = o_halves[:tq] - lam * o_halves[tq:]
    ms = jnp.mean(od * od, axis=-1, keepdims=True)
    return od * lax.rsqrt(ms + RMS_EPS) * subln * out_scale


def _attn_kernel(lam_ref, sub_ref, q_ref, k_ref, v_ref, o_ref, vb_scr, *, tq, tk, out_scale):
    qi = pl.program_id(2)

    @pl.when(qi == 0)
    def _():
        vb_scr[:, :HEAD_W] = v_ref[0]
        vb_scr[:, HEAD_W:] = jnp.ones((vb_scr.shape[0], HEAD_W), BF16)

    q = q_ref[0] * (ATTN_SCALE * LOG2E)
    lane = lax.broadcasted_iota(jnp.int32, (tq, HEAD_W), 1)
    qs = jnp.concatenate([jnp.where(lane < HALF_DIM, q, 0.0), jnp.where(lane >= HALF_DIM, q, 0.0)],
                         axis=0).astype(BF16)

    def scores(j):
        kblk = k_ref[0, pl.ds(pl.multiple_of(j * tk, tk), tk), :]
        return lax.dot_general(qs, kblk, (((1,), (1,)), ((), ())), preferred_element_type=F32)

    def softmax_pv(j, s, m, acc, masked):
        vblk = vb_scr[pl.ds(pl.multiple_of(j * tk, tk), tk), :]
        if masked:
            rq = lax.broadcasted_iota(jnp.int32, (2 * tq, tk), 0)
            rq = jnp.where(rq >= tq, rq - tq, rq) + qi * tq
            ck = lax.broadcasted_iota(jnp.int32, (2 * tq, tk), 1) + j * tk
            s = jnp.where(ck <= rq, s, NEG_BIG)
        m_new = jnp.maximum(m, jnp.max(s, axis=-1, keepdims=True))
        pexp = jnp.exp2((s - m_new).astype(BF16))
        acc = acc * jnp.exp2(m - m_new) + jnp.dot(pexp, vblk, preferred_element_type=F32)
        return m_new, acc

    def body(j, carry):
        s, m, acc = carry
        s_next = scores(j + 1)
        m, acc = softmax_pv(j, s, m, acc, False)
        return s_next, m, acc

    init = (scores(0), jnp.full((2 * tq, 1), NEG_BIG, F32), jnp.zeros((2 * tq, 2 * HEAD_W), F32))
    n_full = (qi * tq) // tk
    s, m, acc = lax.fori_loop(0, n_full, body, init)
    m, acc = softmax_pv(n_full, s, m, acc, True)
    o = acc[:, :HEAD_W] / acc[:, HEAD_W:]
    o_ref[0] = _attn_finish(o, tq, lam_ref[...], sub_ref[...], out_scale).astype(o_ref.dtype)


def _prompt_attention(q, k, v, lam, subln, out_scale, b, s):
    d = q.shape[1]
    tq = min(ATTN_TQ, s)
    tk = min(ATTN_TK, s)
    q3, k3, v3 = (a.reshape(b, s, d) for a in (q, k, v))
    lam_row = jnp.full((1, HEAD_W), lam, F32)
    out = pl.pallas_call(
        functools.partial(_attn_kernel, tq=tq, tk=tk, out_scale=out_scale),
        grid=(b, ATTN_HEADS, s // tq),
        in_specs=[pl.BlockSpec((1, HEAD_W), lambda bi, h, qi: (0, 0)),
                  pl.BlockSpec((1, HEAD_W), lambda bi, h, qi: (0, 0)),
                  pl.BlockSpec((1, tq, HEAD_W), lambda bi, h, qi: (bi, qi, h)),
                  pl.BlockSpec((1, s, HEAD_W), lambda bi, h, qi: (bi, 0, h)),
                  pl.BlockSpec((1, s, HEAD_W), lambda bi, h, qi: (bi, 0, h))],
        out_specs=pl.BlockSpec((1, tq, HEAD_W), lambda bi, h, qi: (bi, qi, h)),
        out_shape=jax.ShapeDtypeStruct((b, s, d), BF16),
        scratch_shapes=[pltpu.VMEM((s, 2 * HEAD_W), BF16)],
        compiler_params=_params("parallel", "parallel", "arbitrary"),
        name="prompt_attention",
    )(lam_row, subln.reshape(1, HEAD_W), q3, k3, v3)
    return out.reshape(b * s, d)


def _paged_kernel(pt_ref, lam_ref, sub_ref, qm_ref, kn_ref, vn_ref, *rest, n_pg, tqn, out_scale):
    k_refs = rest[:n_pg]
    v_refs = rest[n_pg:2 * n_pg]
    o_ref = rest[2 * n_pg]
    m_scr, l_scr, acc_scr = rest[2 * n_pg + 1:]
    si = pl.program_id(1)
    ns = pl.num_programs(1)
    rows = ATTN_HEADS * 2 * tqn
    cols = PAGE_SIZE * ATTN_HEADS

    @pl.when(si == 0)
    def _():
        m_scr[...] = jnp.full_like(m_scr, NEG_BIG)
        l_scr[...] = jnp.zeros_like(l_scr)
        acc_scr[...] = jnp.zeros_like(acc_scr)

    qm = qm_ref[0].astype(BF16)
    nt = (((1,), (1,)), ((), ()))
    same_head = (lax.broadcasted_iota(jnp.int32, (rows, cols), 0) // (2 * tqn)
                 == lax.broadcasted_iota(jnp.int32, (rows, cols), 1) % ATTN_HEADS)
    scores = []
    for g in range(n_pg):
        k2 = k_refs[g][0].reshape(cols, HEAD_W).astype(BF16)
        s = lax.dot_general(qm, k2, nt, preferred_element_type=F32)
        scores.append(jnp.where(same_head, s, NEG_BIG))
    m_old = m_scr[...]
    m_new = m_old
    for s in scores:
        m_new = jnp.maximum(m_new, jnp.max(s, axis=-1, keepdims=True))
    corr = jnp.exp(m_old - m_new)
    l = l_scr[...] * corr
    acc = acc_scr[...] * corr
    for g in range(n_pg):
        pexp = jnp.exp(scores[g] - m_new)
        l = l + jnp.sum(pexp, axis=-1, keepdims=True)
        v2 = v_refs[g][0].reshape(cols, HEAD_W).astype(BF16)
        acc = acc + jnp.dot(pexp.astype(BF16), v2, preferred_element_type=F32)
    m_scr[...] = m_new
    l_scr[...] = l
    acc_scr[...] = acc

    @pl.when(si == ns - 1)
    def _():
        npad = kn_ref.shape[1]
        s = lax.dot_general(qm, kn_ref[0].astype(BF16), nt, preferred_element_type=F32)
        r = lax.broadcasted_iota(jnp.int32, (rows, npad), 0)
        c = lax.broadcasted_iota(jnp.int32, (rows, npad), 1)
        ok = (c < tqn * ATTN_HEADS) & (c % ATTN_HEADS == r // (2 * tqn)) & (c // ATTN_HEADS <= r % tqn)
        s = jnp.where(ok, s, NEG_BIG)
        m1 = jnp.maximum(m_new, jnp.max(s, axis=-1, keepdims=True))
        pexp = jnp.exp(s - m1)
        corr1 = jnp.exp(m_new - m1)
        l1 = l * corr1 + jnp.sum(pexp, axis=-1, keepdims=True)
        acc1 = acc * corr1 + jnp.dot(pexp.astype(BF16), vn_ref[0].astype(BF16), preferred_element_type=F32)
        o = acc1 / l1
        od = o - lam_ref[...] * pltpu.roll(o, rows - tqn, 0)
        ms = jnp.mean(od * od, axis=-1, keepdims=True)
        o_ref[0] = od * lax.rsqrt(ms + RMS_EPS) * sub_ref[...] * out_scale


def _paged_attention(q, k_new, v_new, cache_k, cache_v, page_table, lam, subln, out_scale):
    b, n_pages = page_table.shape
    tqn = q.shape[0] // b
    n_pg = PAGES_PER_STEP
    rows = ATTN_HEADS * 2 * tqn
    q4 = q.reshape(b, tqn, ATTN_HEADS, HEAD_W).transpose(0, 2, 1, 3) * ATTN_SCALE
    lane = jnp.arange(HEAD_W)
    qm = jnp.stack([jnp.where(lane < HALF_DIM, q4, 0.0), jnp.where(lane >= HALF_DIM, q4, 0.0)], axis=2)
    qm = qm.reshape(b, rows, HEAD_W)
    npad = LANES
    kn = jnp.pad(k_new.reshape(b, tqn * ATTN_HEADS, HEAD_W), ((0, 0), (0, npad - tqn * ATTN_HEADS), (0, 0)))
    vn = jnp.pad(v_new.reshape(b, tqn * ATTN_HEADS, HEAD_W), ((0, 0), (0, npad - tqn * ATTN_HEADS), (0, 0)))
    lam_row = jnp.full((1, HEAD_W), lam, F32)

    def page_spec(g):
        return pl.BlockSpec((1, PAGE_SIZE, ATTN_HEADS, HEAD_W),
                            lambda bi, si, pt: (pt[bi, si * n_pg + g], 0, 0, 0))

    per_seq = lambda r: pl.BlockSpec((1, r, HEAD_W), lambda bi, si, pt: (bi, 0, 0))
    grid_spec = pltpu.PrefetchScalarGridSpec(
        num_scalar_prefetch=1,
        grid=(b, n_pages // n_pg),
        in_specs=[pl.BlockSpec((1, HEAD_W), lambda bi, si, pt: (0, 0)),
                  pl.BlockSpec((1, HEAD_W), lambda bi, si, pt: (0, 0)),
                  per_seq(rows), per_seq(npad), per_seq(npad)]
                 + [page_spec(g) for g in range(n_pg)] + [page_spec(g) for g in range(n_pg)],
        out_specs=per_seq(rows),
        scratch_shapes=[pltpu.VMEM((rows, 1), F32), pltpu.VMEM((rows, 1), F32), pltpu.VMEM((rows, HEAD_W), F32)],
    )
    o = pl.pallas_call(
        functools.partial(_paged_kernel, n_pg=n_pg, tqn=tqn, out_scale=out_scale),
        grid_spec=grid_spec,
        out_shape=jax.ShapeDtypeStruct((b, rows, HEAD_W), F32),
        compiler_params=_params("parallel", "arbitrary"),
        name="paged_attention",
    )(page_table, lam_row, subln.reshape(1, HEAD_W), qm, kn, vn, *([cache_k] * n_pg), *([cache_v] * n_pg))
    o = o.reshape(b, ATTN_HEADS, 2, tqn, HEAD_W)[:, :, 0]
    return o.transpose(0, 2, 1, 3).reshape(b * tqn, ATTN_HEADS * HEAD_W).astype(BF16)


def _trunk(x, conv_buf, ssm_state, p, attention):
    b, l, d = x.shape
    t = b * l
    x, new_conv, new_state = _mamba_layer(x, conv_buf, ssm_state, p, 0)
    x = _moe_layer(x.reshape(t, d), p, 0)
    k, v, k_b, v_b = _norm_matmul(x, p['kv_norm'], [p['w_k'].astype(BF16), p['w_v'].astype(BF16)], per_head=True)
    layer = N_A_LAYERS
    lam_init = 0.8 - 0.6 * math.exp(-0.3 * layer)
    lq = p['attn_lambda'][0]
    lam = jnp.exp(jnp.sum(lq[0] * lq[1])) - jnp.exp(jnp.sum(lq[2] * lq[3])) + lam_init
    q, = _norm_matmul(x, p['attn_norm'][0], [p['attn_w_q'][0].astype(BF16)])
    o = attention(q, k, v, k_b, v_b, lam, p['attn_subln'][0], 1.0 - lam_init)
    x = _matmul_res(o, p['attn_w_o'][0].astype(BF16), x)
    y = _moe_layer(x, p, layer, final_gain=p['final_norm'])
    kv_shape = (b, l, ATTN_HEADS, HEAD_W)
    return y.reshape(b, l, d), new_conv[None], new_state[None], k.reshape(kv_shape), v.reshape(kv_shape)


def kernel(x_prompt, x_sample, state_ssm, state_conv, cache_k, cache_v, page_table, ssm_norm, ssm_w_in, ssm_conv_w, ssm_conv_b, ssm_dt_bias, ssm_a_log, ssm_d, ssm_gate_norm, ssm_w_out, kv_norm, w_k, w_v, attn_norm, attn_w_q, attn_lambda, attn_subln, attn_w_o, moe_norm, moe_w_group, moe_w_expert, moe_w_gate, moe_w_up, moe_w_down, final_norm):
    p = dict(ssm_norm=ssm_norm, ssm_w_in=ssm_w_in, ssm_conv_w=ssm_conv_w, ssm_conv_b=ssm_conv_b,
             ssm_dt_bias=ssm_dt_bias, ssm_a_log=ssm_a_log, ssm_d=ssm_d, ssm_gate_norm=ssm_gate_norm,
             ssm_w_out=ssm_w_out, kv_norm=kv_norm, w_k=w_k, w_v=w_v, attn_norm=attn_norm,
             attn_w_q=attn_w_q, attn_lambda=attn_lambda, attn_subln=attn_subln, attn_w_o=attn_w_o,
             moe_norm=moe_norm, moe_w_group=moe_w_group, moe_w_expert=moe_w_expert,
             moe_w_gate=moe_w_gate, moe_w_up=moe_w_up, moe_w_down=moe_w_down, final_norm=final_norm)
    bp, sp, _ = x_prompt.shape
    conv0 = jnp.zeros((bp, CONV_W - 1, CONV_DIM), F32)
    ssm0 = jnp.zeros((bp, SSM_HEADS, SSM_HEAD_DIM, D_STATE), F32)
    def prompt_attn(q, k, v, k_b, v_b, lam, subln, out_scale):
        return _prompt_attention(q, k_b, v_b, lam, subln, out_scale, b=bp, s=sp)

    y_p, conv_p, ssm_p, k_p, v_p = _trunk(x_prompt, conv0, ssm0, p, prompt_attn)

    def sample_attn(q, k, v, k_b, v_b, lam, subln, out_scale):
        return _paged_attention(q, k, v, cache_k, cache_v, page_table, lam, subln, out_scale)

    y_s, conv_s, ssm_s, k_s, v_s = _trunk(x_sample, state_conv[0], state_ssm[0], p, sample_attn)
    return (y_p, y_s, ssm_p, conv_p, k_p, v_p, ssm_s, conv_s, k_s, v_s)
```

```python
import functools
import math

import jax
import jax.numpy as jnp
from jax import lax
from jax.experimental import pallas as pl
from jax.experimental.pallas import tpu as pltpu

F32 = jnp.float32
BF16 = jnp.bfloat16

D_MODEL = 1024
DEPTH = 2
N_A_LAYERS = DEPTH // 2
D_INNER = 2 * D_MODEL
SSM_HEAD_DIM = 64
SSM_HEADS = D_INNER // SSM_HEAD_DIM
SSM_GROUPS = 4
HEADS_PER_GROUP = SSM_HEADS // SSM_GROUPS
D_STATE = 128
CONV_W = 4
CONV_DIM = D_INNER + 2 * SSM_GROUPS * D_STATE
ZX_DIM = D_INNER + CONV_DIM
SSD_CHUNK = 128
GATED_NORM_EPS = 1e-5
HALF_DIM = 64
ATTN_HEADS = D_MODEL // (2 * HALF_DIM)
HEAD_W = 2 * HALF_DIM
ATTN_SCALE = HALF_DIM ** -0.5
N_EGROUPS = 4
EXPERTS_PER_GROUP = 8
N_EXPERTS = N_EGROUPS * EXPERTS_PER_GROUP
TOP_K = 2
EXPERT_FF = D_MODEL // 2
RMS_EPS = 1e-6
PAGE_SIZE = 128

LANES = 128
SUBLANES = 8
MOE_ROWS = 512
MOE_ROWS_SMALL = 128
ATTN_TQ = 512
ATTN_TK = 512
PAGES_PER_STEP = 16
COL_CHUNK = 1024
LOG2E = 1.4426950408889634
NEG_BIG = -1e30
VMEM_LIMIT = 56 * 1024 * 1024


def _params(*sem):
    return pltpu.CompilerParams(dimension_semantics=sem, vmem_limit_bytes=VMEM_LIMIT)


def _silu(x):
    return x * jax.nn.sigmoid(x)


def _split3(x):
    hi = x.astype(BF16)
    r1 = x - hi.astype(F32)
    mid = r1.astype(BF16)
    lo = (r1 - mid.astype(F32)).astype(BF16)
    return jnp.concatenate([hi, mid, lo], axis=1)


def _norm_matmul_kernel(*refs, n_w, has_small, per_head):
    x_ref, g_ref = refs[0], refs[1]
    w_refs = refs[2:2 + n_w]
    pos = 2 + n_w
    ws_ref = refs[pos] if has_small else None
    pos += int(has_small)
    o_refs = refs[pos:pos + n_w]
    pos += n_w
    os_ref = refs[pos] if has_small else None
    pos += int(has_small)
    ob_refs = refs[pos:pos + n_w] if per_head else None

    x = x_ref[...]
    ms = jnp.mean(x * x, axis=-1, keepdims=True)
    h = (x * lax.rsqrt(ms + RMS_EPS) * g_ref[...]).astype(BF16)
    if has_small:
        os_ref[...] = jnp.dot(h, ws_ref[...], preferred_element_type=F32)
    for k, (w_ref, o_ref) in enumerate(zip(w_refs, o_refs)):
        n = w_ref.shape[1]
        step = min(COL_CHUNK, n)
        for c in range(0, n, step):
            res = jnp.dot(h, w_ref[:, c:c + step], preferred_element_type=F32)
            if per_head:
                for hh in range(step // HEAD_W):
                    o_ref[:, c // HEAD_W + hh, :] = res[:, hh * HEAD_W:(hh + 1) * HEAD_W]
                ob_refs[k][:, c:c + step] = res.astype(BF16)
            else:
                o_ref[:, c:c + step] = res


def _resident(shape):
    return pl.BlockSpec(shape, lambda i: (0,) * len(shape), pipeline_mode=pl.Buffered(1))


def _norm_matmul(x, gain, weights, w_small=None, per_head=False, tm=512):
    t, d = x.shape
    n = weights[0].shape[1]
    tm = min(tm, t)
    n_w = len(weights)
    has_small = w_small is not None
    in_specs = [pl.BlockSpec((tm, d), lambda i: (i, 0)), _resident((1, d))]
    in_specs += [_resident((d, n))] * n_w
    if per_head:
        out_shape = [jax.ShapeDtypeStruct((t, n // HEAD_W, HEAD_W), F32)] * n_w
        out_specs = [pl.BlockSpec((tm, n // HEAD_W, HEAD_W), lambda i: (i, 0, 0))] * n_w
    else:
        out_shape = [jax.ShapeDtypeStruct((t, n), F32)] * n_w
        out_specs = [pl.BlockSpec((tm, n), lambda i: (i, 0))] * n_w
    args = [x, gain.reshape(1, d)] + list(weights)
    if has_small:
        in_specs.append(_resident((d, LANES)))
        out_shape.append(jax.ShapeDtypeStruct((t, LANES), F32))
        out_specs.append(pl.BlockSpec((tm, LANES), lambda i: (i, 0)))
        args.append(w_small)
    if per_head:
        out_shape += [jax.ShapeDtypeStruct((t, n), BF16)] * n_w
        out_specs += [pl.BlockSpec((tm, n), lambda i: (i, 0))] * n_w
    return pl.pallas_call(
        functools.partial(_norm_matmul_kernel, n_w=n_w, has_small=has_small, per_head=per_head),
        grid=(t // tm,),
        in_specs=in_specs, out_specs=out_specs, out_shape=out_shape,
        compiler_params=_params("parallel"),
        name="norm_matmul",
    )(*args)


def _matmul_res_kernel(a_ref, w_ref, r_ref, o_ref):
    o_ref[...] = r_ref[...] + jnp.dot(a_ref[...], w_ref[...], preferred_element_type=F32)


def _matmul_res(a, w, res, tm=512):
    t, k = a.shape
    n = w.shape[1]
    tm = min(tm, t)
    return pl.pallas_call(
        _matmul_res_kernel,
        grid=(t // tm,),
        in_specs=[pl.BlockSpec((tm, k), lambda i: (i, 0)), _resident((k, n)),
                  pl.BlockSpec((tm, n), lambda i: (i, 0))],
        out_specs=pl.BlockSpec((tm, n), lambda i: (i, 0)),
        out_shape=jax.ShapeDtypeStruct((t, n), F32),
        compiler_params=_params("parallel"),
        name="matmul_res",
    )(a, w, res)


def _ssd_kernel(zx_ref, dt_ref, cbuf_ref, h0_ref, convw_ref, convb_ref, dtb_ref, alog_ref, dfull_ref,
                gnorm_ref, tri_ref, e_ref, e2_ref, g_ref, ht_ref, state_scr, tail_scr, *, tc, valid):
    ci = pl.program_id(1)
    nc = pl.num_programs(1)
    gw = HEADS_PER_GROUP * SSM_HEAD_DIM
    n_pairs = HEADS_PER_GROUP // 2

    @pl.when(ci == 0)
    def _init():
        tail_scr[...] = cbuf_ref[0]
        for g in range(SSM_GROUPS):
            for j in range(n_pairs):
                h2 = h0_ref[0, pl.ds(g * HEADS_PER_GROUP + 2 * j, 2)]
                state_scr[g, :, j * LANES:(j + 1) * LANES] = h2.reshape(2 * SSM_HEAD_DIM, D_STATE).T

    xr = zx_ref[0, :, D_INNER:ZX_DIM]
    xcat = jnp.concatenate([tail_scr[...], xr], axis=0)
    conv = convb_ref[...] + convw_ref[CONV_W - 1:CONV_W, :] * xr
    for k in range(1, CONV_W):
        conv = conv + convw_ref[CONV_W - 1 - k:CONV_W - k, :] * xcat[SUBLANES - k:SUBLANES - k + tc]
    tail_scr[...] = xr[tc - SUBLANES:tc]
    xbc = _silu(conv)
    xs = xbc[:, :D_INNER]
    bm = xbc[:, D_INNER:D_INNER + SSM_GROUPS * D_STATE]
    cm = xbc[:, D_INNER + SSM_GROUPS * D_STATE:]

    dtl = dt_ref[0] + dtb_ref[...]
    dt = jnp.maximum(dtl, 0.0) + jnp.log1p(jnp.exp(-jnp.abs(dtl)))
    row = lax.broadcasted_iota(jnp.int32, (tc, LANES), 0)
    if valid is not None:
        dt = jnp.where(row < valid, dt, 0.0)
    la = dt * (-jnp.exp(alog_ref[...]))
    cs3 = jnp.dot(tri_ref[...], _split3(la), preferred_element_type=F32)
    cs = cs3[:, :LANES] + cs3[:, LANES:2 * LANES] + cs3[:, 2 * LANES:]
    cs_last = cs[tc - 1:tc, :]
    ecs = jnp.exp(cs)
    to_end = jnp.exp(cs_last - cs)
    cdecay = jnp.exp(jnp.broadcast_to(cs_last, (SUBLANES, LANES)))

    lane_id = lax.broadcasted_iota(jnp.int32, (tc, LANES), 1)

    def expand(v, e):
        hi = v.astype(BF16).astype(F32)
        r1 = v - hi
        mid = r1.astype(BF16).astype(F32)
        lo = r1 - mid
        lanes = lane_id[:v.shape[0]]
        packed = jnp.where(lanes < SSM_HEADS, hi,
                           jnp.where(lanes < 2 * SSM_HEADS, pltpu.roll(mid, SSM_HEADS, 1),
                                     pltpu.roll(lo, 2 * SSM_HEADS, 1)))
        return jnp.dot(packed.astype(BF16), e, preferred_element_type=F32)

    e_mat = e_ref[...]
    dt_full = expand(dt, e_mat)
    ecs_full = expand(ecs, e_mat)
    toend_full = expand(to_end, e_mat)
    cdecay_full = expand(cdecay, e_mat)[0:1, :]
    cs_col = expand(cs, e2_ref[...])
    cs_row = cs.T

    xdt = xs * dt_full
    xdt_b = xdt.astype(BF16)
    xw_b = (xdt * toend_full).astype(BF16)
    z = zx_ref[0, :, :D_INNER]

    tri_mask = lax.broadcasted_iota(jnp.int32, (tc, tc), 0) >= lax.broadcasted_iota(jnp.int32, (tc, tc), 1)
    lane_lo = lax.broadcasted_iota(jnp.int32, (tc, LANES), 1) < SSM_HEAD_DIM

    for g in range(SSM_GROUPS):
        gs = slice(g * gw, (g + 1) * gw)
        b_g = bm[:, g * D_STATE:(g + 1) * D_STATE]
        c_g = cm[:, g * D_STATE:(g + 1) * D_STATE].astype(BF16)
        cb = lax.dot_general(c_g, b_g.astype(BF16), (((1,), (1,)), ((), ())), preferred_element_type=F32)
        bt_g = b_g.T.astype(BF16)
        st_old = state_scr[g]
        y_off = jnp.dot(c_g, st_old.astype(BF16), preferred_element_type=F32)
        s_new = jnp.dot(bt_g, xw_b[:, gs], preferred_element_type=F32)
        state_scr[g] = st_old * cdecay_full[:, gs] + s_new
        pairs = []
        for j in range(n_pairs):
            x_pair = xdt_b[:, g * gw + j * LANES:g * gw + (j + 1) * LANES]
            acc = None
            for r2 in range(2):
                h = g * HEADS_PER_GROUP + 2 * j + r2
                seg = cs_col[:, h * LANES:(h + 1) * LANES] - cs_row[h:h + 1, :]
                m = (jnp.exp(jnp.where(tri_mask, seg, NEG_BIG)) * cb).astype(BF16)
                keep = lane_lo if r2 == 0 else jnp.logical_not(lane_lo)
                rhs = jnp.where(keep, x_pair, jnp.zeros_like(x_pair))
                part = jnp.dot(m, rhs, preferred_element_type=F32)
                acc = part if acc is None else acc + part
            pairs.append(acc)
        y = jnp.concatenate(pairs, axis=1) + y_off * ecs_full[:, gs] + xs[:, gs] * dfull_ref[:, gs]
        gated = y * _silu(z[:, gs])
        ms = jnp.mean(gated * gated, axis=-1, keepdims=True)
        g_ref[0, :, gs] = (gated * lax.rsqrt(ms + GATED_NORM_EPS) * gnorm_ref[:, gs]).astype(g_ref.dtype)

    @pl.when(ci == nc - 1)
    def _fin():
        for g in range(SSM_GROUPS):
            for j in range(n_pairs):
                st = state_scr[g, :, j * LANES:(j + 1) * LANES].T
                ht_ref[0, pl.ds(g * HEADS_PER_GROUP + 2 * j, 2)] = st.reshape(2, SSM_HEAD_DIM, D_STATE)


def _ssd_constants(tc):
    tri = (jnp.arange(tc)[:, None] >= jnp.arange(tc)[None, :]).astype(BF16)
    head = jnp.arange(LANES)

    def expansion(width):
        cols = jnp.arange(SSM_HEADS * width) // width
        piece = head % SSM_HEADS
        return ((piece[:, None] == cols[None, :]) & (head[:, None] < 3 * SSM_HEADS)).astype(BF16)

    return tri, expansion(SSM_HEAD_DIM), expansion(LANES)


def _ssd(zx, dtr, cbuf8, h0, conv_w, conv_b, dt_bias, a_log, d_skip, gate_norm, valid):
    b, lp, _ = zx.shape
    tc = SSD_CHUNK
    nc = lp // tc
    tri, e_mat, e2_mat = _ssd_constants(tc)
    pad = LANES - SSM_HEADS
    dtb = jnp.pad(dt_bias, (0, pad)).reshape(1, LANES)
    alog = jnp.pad(a_log, (0, pad)).reshape(1, LANES)
    dfull = jnp.repeat(d_skip, SSM_HEAD_DIM).reshape(1, D_INNER)
    const = lambda *shape: pl.BlockSpec(shape, lambda bi, ci: (0,) * len(shape))
    return pl.pallas_call(
        functools.partial(_ssd_kernel, tc=tc, valid=valid),
        grid=(b, nc),
        in_specs=[pl.BlockSpec((1, tc, ZX_DIM), lambda bi, ci: (bi, ci, 0)),
                  pl.BlockSpec((1, tc, LANES), lambda bi, ci: (bi, ci, 0)),
                  pl.BlockSpec((1, SUBLANES, CONV_DIM), lambda bi, ci: (bi, 0, 0)),
                  pl.BlockSpec((1, SSM_HEADS, SSM_HEAD_DIM, D_STATE), lambda bi, ci: (bi, 0, 0, 0)),
                  const(CONV_W, CONV_DIM), const(1, CONV_DIM), const(1, LANES), const(1, LANES),
                  const(1, D_INNER), const(1, D_INNER),
                  const(tc, tc), const(LANES, D_INNER), const(LANES, SSM_HEADS * LANES)],
        out_specs=[pl.BlockSpec((1, tc, D_INNER), lambda bi, ci: (bi, ci, 0)),
                   pl.BlockSpec((1, SSM_HEADS, SSM_HEAD_DIM, D_STATE), lambda bi, ci: (bi, 0, 0, 0))],
        out_shape=[jax.ShapeDtypeStruct((b, lp, D_INNER), BF16),
                   jax.ShapeDtypeStruct((b, SSM_HEADS, SSM_HEAD_DIM, D_STATE), F32)],
        scratch_shapes=[pltpu.VMEM((SSM_GROUPS, D_STATE, HEADS_PER_GROUP * SSM_HEAD_DIM), F32),
                        pltpu.VMEM((SUBLANES, CONV_DIM), F32)],
        compiler_params=_params("parallel", "arbitrary"),
        name="ssd",
    )(zx, dtr, cbuf8, h0, conv_w, conv_b.reshape(1, CONV_DIM), dtb, alog, dfull,
      gate_norm.reshape(1, D_INNER), tri, e_mat, e2_mat)


def _mamba_layer(x, conv_buf, ssm_state, p, j):
    b, l, d = x.shape
    t = b * l
    w_in = p['ssm_w_in'][j]
    w_zx = w_in[:, :ZX_DIM].astype(BF16)
    w_dt = jnp.pad(w_in[:, ZX_DIM:], ((0, 0), (0, LANES - SSM_HEADS))).astype(BF16)
    zx, dtr = _norm_matmul(x.reshape(t, d), p['ssm_norm'][j], [w_zx], w_small=w_dt)
    zx = zx.reshape(b, l, ZX_DIM)
    dtr = dtr.reshape(b, l, LANES)
    xbc_raw = zx[:, :, D_INNER:]
    new_conv = jnp.concatenate([conv_buf, xbc_raw[:, max(l - (CONV_W - 1), 0):]], axis=1)[:, -(CONV_W - 1):]
    lp = -(-l // SSD_CHUNK) * SSD_CHUNK
    valid = None
    if lp != l:
        valid = l
        zx = jnp.pad(zx, ((0, 0), (0, lp - l), (0, 0)))
        dtr = jnp.pad(dtr, ((0, 0), (0, lp - l), (0, 0)))
    cbuf8 = jnp.pad(conv_buf, ((0, 0), (SUBLANES - (CONV_W - 1), 0), (0, 0)))
    gated, new_state = _ssd(zx, dtr, cbuf8, ssm_state, p['ssm_conv_w'][j], p['ssm_conv_b'][j],
                            p['ssm_dt_bias'][j], p['ssm_a_log'][j], p['ssm_d'][j], p['ssm_gate_norm'][j], valid)
    gated = gated[:, :l].reshape(t, D_INNER)
    x_new = _matmul_res(gated, p['ssm_w_out'][j].astype(BF16), x.reshape(t, d))
    return x_new.reshape(b, l, d), new_conv, new_state


def _router_kernel(x_ref, g_ref, wr_ref, low_ref, h_ref, route_ref, cnt_ref, route_t_ref, carry_scr):
    @pl.when(pl.program_id(0) == 0)
    def _():
        carry_scr[...] = jnp.zeros_like(carry_scr)

    x = x_ref[...]
    ms = jnp.mean(x * x, axis=-1, keepdims=True)
    h = x * lax.rsqrt(ms + RMS_EPS) * g_ref[...]
    h1 = h.astype(BF16)
    h_ref[...] = h1
    h2 = (h - h1.astype(F32)).astype(BF16)
    logits = jnp.dot(jnp.concatenate([h1, h1, h2], axis=1), wr_ref[...], preferred_element_type=F32)
    tm = x.shape[0]
    lane = lax.broadcasted_iota(jnp.int32, (tm, LANES), 1)
    gl = jnp.where(lane < N_EGROUPS, logits, NEG_BIG)
    gmax = jnp.max(gl, axis=-1, keepdims=True)
    g_sel = jnp.min(jnp.where(gl == gmax, lane, LANES), axis=-1, keepdims=True)
    p_group = 1.0 / jnp.sum(jnp.exp(gl - gmax), axis=-1, keepdims=True)
    lo = N_EGROUPS + EXPERTS_PER_GROUP * g_sel
    el = jnp.where((lane >= lo) & (lane < lo + EXPERTS_PER_GROUP), logits, NEG_BIG)
    v1 = jnp.max(el, axis=-1, keepdims=True)
    i1 = jnp.min(jnp.where(el == v1, lane, LANES), axis=-1, keepdims=True)
    el2 = jnp.where(lane == i1, NEG_BIG, el)
    v2 = jnp.max(el2, axis=-1, keepdims=True)
    i2 = jnp.min(jnp.where(el2 == v2, lane, LANES), axis=-1, keepdims=True)
    e2 = jnp.exp(v2 - v1)
    gate1 = p_group / (1.0 + e2)
    gate2 = p_group * e2 / (1.0 + e2)
    hit1 = lane == i1
    hit2 = lane == i2
    sel = jnp.where(hit1 | hit2, 1.0, 0.0)
    before = jnp.dot(low_ref[...], sel.astype(BF16), preferred_element_type=F32) + carry_scr[0:1, :]
    rank1 = jnp.sum(jnp.where(hit1, before, 0.0), axis=-1, keepdims=True)
    rank2 = jnp.sum(jnp.where(hit2, before, 0.0), axis=-1, keepdims=True)
    carry_scr[...] = carry_scr[...] + jnp.sum(sel, axis=0, keepdims=True)
    cnt_ref[...] = carry_scr[...]
    vals = ((i1 - N_EGROUPS).astype(F32), (i2 - N_EGROUPS).astype(F32), gate1, gate2, rank1, rank2)
    out = jnp.zeros((tm, LANES), F32)
    for k, v in enumerate(vals):
        out = jnp.where(lane == k, v, out)
    route_ref[...] = out
    route_t_ref[...] = out.T[:SUBLANES, :]


def _router(x, gain, w_group, w_expert, layer, tm=512):
    t, d = x.shape
    tm = min(tm, t)
    wr = jnp.concatenate([w_group, w_expert], axis=2)
    wr = jnp.pad(wr, ((0, 0), (0, 0), (0, LANES - wr.shape[2])))
    w1 = wr.astype(BF16)
    w2 = (wr - w1.astype(F32)).astype(BF16)
    wr3 = jnp.concatenate([w1, w2, w1], axis=1)[layer]
    strict_lower = (jnp.arange(tm)[:, None] > jnp.arange(tm)[None, :]).astype(BF16)
    return pl.pallas_call(
        _router_kernel,
        grid=(t // tm,),
        in_specs=[pl.BlockSpec((tm, d), lambda i: (i, 0)), _resident((1, d)), _resident((3 * d, LANES)),
                  _resident((tm, tm))],
        out_specs=[pl.BlockSpec((tm, d), lambda i: (i, 0)),
                   pl.BlockSpec((tm, LANES), lambda i: (i, 0)),
                   pl.BlockSpec((SUBLANES, LANES), lambda i: (0, 0)),
                   pl.BlockSpec((SUBLANES, tm), lambda i: (0, i))],
        out_shape=[jax.ShapeDtypeStruct((t, d), BF16), jax.ShapeDtypeStruct((t, LANES), F32),
                   jax.ShapeDtypeStruct((SUBLANES, LANES), F32), jax.ShapeDtypeStruct((SUBLANES, t), F32)],
        scratch_shapes=[pltpu.VMEM((SUBLANES, LANES), F32)],
        compiler_params=_params("arbitrary"),
        name="moe_router",
    )(x, gain.reshape(1, d), wr3, strict_lower)


def _grouped_mlp_kernel(be_ref, nu_ref, xs_ref, wg_ref, wu_ref, wd_ref, o_ref, wg_s, wu_s, wd_s):
    b = pl.program_id(0)
    changed = jnp.logical_or(b == 0, be_ref[b] != be_ref[jnp.maximum(b - 1, 0)])

    @pl.when(changed)
    def _():
        wg_s[...] = wg_ref[0, 0].astype(BF16)
        wu_s[...] = wu_ref[0, 0].astype(BF16)
        wd_s[...] = wd_ref[0, 0].astype(BF16)

    @pl.when(b < nu_ref[0])
    def _():
        x = xs_ref[...]
        gate = jnp.dot(x, wg_s[...], preferred_element_type=F32)
        up = jnp.dot(x, wu_s[...], preferred_element_type=F32)
        hid = (_silu(gate) * up).astype(BF16)
        o_ref[...] = jnp.dot(hid, wd_s[...], preferred_element_type=F32).astype(o_ref.dtype)

    @pl.when(b >= nu_ref[0])
    def _():
        o_ref[...] = jnp.zeros_like(o_ref)


def _grouped_mlp(xs, block_expert, n_used, w_gate, w_up, w_down, layer, rows):
    n_slots, d = xs.shape
    ff = w_gate.shape[3]
    n_blocks = n_slots // rows
    grid_spec = pltpu.PrefetchScalarGridSpec(
        num_scalar_prefetch=2,
        grid=(n_blocks,),
        in_specs=[pl.BlockSpec((rows, d), lambda b, be, nu: (b, 0)),
                  pl.BlockSpec((1, 1, d, ff), lambda b, be, nu: (layer, be[b], 0, 0)),
                  pl.BlockSpec((1, 1, d, ff), lambda b, be, nu: (layer, be[b], 0, 0)),
                  pl.BlockSpec((1, 1, ff, d), lambda b, be, nu: (layer, be[b], 0, 0))],
        out_specs=pl.BlockSpec((rows, d), lambda b, be, nu: (b, 0)),
        scratch_shapes=[pltpu.VMEM((d, ff), BF16), pltpu.VMEM((d, ff), BF16), pltpu.VMEM((ff, d), BF16)],
    )
    return pl.pallas_call(
        _grouped_mlp_kernel,
        grid_spec=grid_spec,
        out_shape=jax.ShapeDtypeStruct((n_slots, d), BF16),
        compiler_params=_params("arbitrary"),
        name="moe_experts",
    )(block_expert, n_used, xs, w_gate, w_up, w_down)


def _combine_kernel(x_ref, y0_ref, y1_ref, route_ref, gain_ref, o_ref, *, final):
    r = route_ref[...]
    xn = x_ref[...] + r[:, 2:3] * y0_ref[...].astype(F32) + r[:, 3:4] * y1_ref[...].astype(F32)
    if final:
        ms = jnp.mean(xn * xn, axis=-1, keepdims=True)
        xn = xn * lax.rsqrt(ms + RMS_EPS) * gain_ref[...]
    o_ref[...] = xn


def _combine(x, y01, route, final_gain, tm=512):
    t, d = x.shape
    tm = min(tm, t)
    final = final_gain is not None
    gain = (final_gain if final else jnp.ones((d,), F32)).reshape(1, d)
    row = pl.BlockSpec((tm, d), lambda i: (i, 0))
    return pl.pallas_call(
        functools.partial(_combine_kernel, final=final),
        grid=(t // tm,),
        in_specs=[row, row, pl.BlockSpec((tm, d), lambda i: (i + t // tm, 0)),
                  pl.BlockSpec((tm, LANES), lambda i: (i, 0)), _resident((1, d))],
        out_specs=row,
        out_shape=jax.ShapeDtypeStruct((t, d), F32),
        compiler_params=_params("parallel"),
        name="moe_combine",
    )(x, y01, y01, route, gain)


def _moe_layer(x, p, i, final_gain=None):
    t, d = x.shape
    hb, route, cnt, route_t = _router(x, p['moe_norm'][i], p['moe_w_group'], p['moe_w_expert'], i)
    a = t * TOP_K
    r = MOE_ROWS if a >= N_EXPERTS * MOE_ROWS else MOE_ROWS_SMALL
    i32 = jnp.int32
    e_idx = route_t[:TOP_K].astype(i32)
    rank = route_t[4:4 + TOP_K].astype(i32)
    counts = cnt[0, N_EGROUPS:N_EGROUPS + N_EXPERTS].astype(i32)
    padded = (counts + r - 1) // r * r
    pad_end = jnp.cumsum(padded)
    pad_start = pad_end - padded
    start = jnp.cumsum(counts) - counts
    experts = jnp.arange(N_EXPERTS, dtype=i32)
    seg_start = jnp.sum(jnp.where(e_idx[:, None, :] == experts[None, :, None], pad_start[None, :, None], 0), axis=1)
    dest = (seg_start + rank).reshape(a)
    n_blocks = (a + N_EXPERTS * (r - 1)) // r
    n_used = (pad_end[-1] // r).astype(i32)
    blk = jnp.arange(n_blocks, dtype=i32)
    block_expert = jnp.minimum(jnp.searchsorted(pad_end, blk * r, side='right'), N_EXPERTS - 1).astype(i32)
    order = jnp.argsort(e_idx.T.reshape(a), stable=True).astype(i32)
    off = blk[:, None] * r + jnp.arange(r, dtype=i32)[None, :] - pad_start[block_expert][:, None]
    live = (off < counts[block_expert][:, None]) & (blk < n_used)[:, None]
    src = jnp.clip(start[block_expert][:, None] + off, 0, a - 1)
    spread = (blk[:, None] * r + jnp.arange(r, dtype=i32)[None, :]) % t
    slot_tok = jnp.where(live, order[src.reshape(-1)].reshape(n_blocks, r) // TOP_K, spread).reshape(n_blocks * r)
    last_e = block_expert[jnp.maximum(n_used - 1, 0)]
    block_expert = jnp.where(blk < n_used, block_expert, last_e)
    xs = hb[slot_tok]
    yb = _grouped_mlp(xs, block_expert, n_used.reshape(1), p['moe_w_gate'], p['moe_w_up'], p['moe_w_down'], i, r)
    return _combine(x, yb[dest], route, final_gain)


def _attn_finish(o_halves, tq, lam, subln, out_scale):
    od = o_halves[:tq] - lam * o_halves[tq:]
    ms = jnp.mean(od * od, axis=-1, keepdims=True)
    return od * lax.rsqrt(ms + RMS_EPS) * subln * out_scale


def _attn_kernel(lam_ref, sub_ref, q_ref, k_ref, v_ref, o_ref, vb_scr, *, tq, tk, out_scale):
    qi = pl.program_id(2)

    @pl.when(qi == 0)
    def _():
        vb_scr[:, :HEAD_W] = v_ref[0]
        vb_scr[:, HEAD_W:] = jnp.ones((vb_scr.shape[0], HEAD_W), BF16)

    q = q_ref[0] * (ATTN_SCALE * LOG2E)
    lane = lax.broadcasted_iota(jnp.int32, (tq, HEAD_W), 1)
    qs = jnp.concatenate([jnp.where(lane < HALF_DIM, q, 0.0), jnp.where(lane >= HALF_DIM, q, 0.0)],
                         axis=0).astype(BF16)

    def scores(j):
        kblk = k_ref[0, pl.ds(pl.multiple_of(j * tk, tk), tk), :]
        return lax.dot_general(qs, kblk, (((1,), (1,)), ((), ())), preferred_element_type=F32)

    def softmax_pv(j, s, m, acc, masked):
        vblk = vb_scr[pl.ds(pl.multiple_of(j * tk, tk), tk), :]
        if masked:
            rq = lax.broadcasted_iota(jnp.int32, (2 * tq, tk), 0)
            rq = jnp.where(rq >= tq, rq - tq, rq) + qi * tq
            ck = lax.broadcasted_iota(jnp.int32, (2 * tq, tk), 1) + j * tk
            s = jnp.where(ck <= rq, s, NEG_BIG)
        m_new = jnp.maximum(m, jnp.max(s, axis=-1, keepdims=True))
        pexp = jnp.exp2((s - m_new).astype(BF16))
        acc = acc * jnp.exp2(m - m_new) + jnp.dot(pexp, vblk, preferred_element_type=F32)
        return m_new, acc

    def body(j, carry):
        s, m, acc = carry
        s_next = scores(j + 1)
        m, acc = softmax_pv(j, s, m, acc, False)
        return s_next, m, acc

    init = (scores(0), jnp.full((2 * tq, 1), NEG_BIG, F32), jnp.zeros((2 * tq, 2 * HEAD_W), F32))
    n_full = (qi * tq) // tk
    s, m, acc = lax.fori_loop(0, n_full, body, init)
    m, acc = softmax_pv(n_full, s, m, acc, True)
    o = acc[:, :HEAD_W] / acc[:, HEAD_W:]
    o_ref[0] = _attn_finish(o, tq, lam_ref[...], sub_ref[...], out_scale).astype(o_ref.dtype)


def _prompt_attention(q, k, v, lam, subln, out_scale, b, s):
    d = q.shape[1]
    tq = min(ATTN_TQ, s)
    tk = min(ATTN_TK, s)
    q3, k3, v3 = (a.reshape(b, s, d) for a in (q, k, v))
    lam_row = jnp.full((1, HEAD_W), lam, F32)
    out = pl.pallas_call(
        functools.partial(_attn_kernel, tq=tq, tk=tk, out_scale=out_scale),
        grid=(b, ATTN_HEADS, s // tq),
        in_specs=[pl.BlockSpec((1, HEAD_W), lambda bi, h, qi: (0, 0)),
                  pl.BlockSpec((1, HEAD_W), lambda bi, h, qi: (0, 0)),
                  pl.BlockSpec((1, tq, HEAD_W), lambda bi, h, qi: (bi, qi, h)),
                  pl.BlockSpec((1, s, HEAD_W), lambda bi, h, qi: (bi, 0, h)),
                  pl.BlockSpec((1, s, HEAD_W), lambda bi, h, qi: (bi, 0, h))],
        out_specs=pl.BlockSpec((1, tq, HEAD_W), lambda bi, h, qi: (bi, qi, h)),
        out_shape=jax.ShapeDtypeStruct((b, s, d), BF16),
        scratch_shapes=[pltpu.VMEM((s, 2 * HEAD_W), BF16)],
        compiler_params=_params("parallel", "parallel", "arbitrary"),
        name="prompt_attention",
    )(lam_row, subln.reshape(1, HEAD_W), q3, k3, v3)
    return out.reshape(b * s, d)


def _paged_kernel(pt_ref, lam_ref, sub_ref, qm_ref, kn_ref, vn_ref, *rest, n_pg, tqn, out_scale):
    k_refs = rest[:n_pg]
    v_refs = rest[n_pg:2 * n_pg]
    o_ref = rest[2 * n_pg]
    m_scr, l_scr, acc_scr = rest[2 * n_pg + 1:]
    si = pl.program_id(1)
    ns = pl.num_programs(1)
    rows = ATTN_HEADS * 2 * tqn
    cols = PAGE_SIZE * ATTN_HEADS

    @pl.when(si == 0)
    def _():
        m_scr[...] = jnp.full_like(m_scr, NEG_BIG)
        l_scr[...] = jnp.zeros_like(l_scr)
        acc_scr[...] = jnp.zeros_like(acc_scr)

    qm = qm_ref[0].astype(BF16)
    nt = (((1,), (1,)), ((), ()))
    same_head = (lax.broadcasted_iota(jnp.int32, (rows, cols), 0) // (2 * tqn)
                 == lax.broadcasted_iota(jnp.int32, (rows, cols), 1) % ATTN_HEADS)
    scores = []
    for g in range(n_pg):
        k2 = k_refs[g][0].reshape(cols, HEAD_W).astype(BF16)
        s = lax.dot_general(qm, k2, nt, preferred_element_type=F32)
        scores.append(jnp.where(same_head, s, NEG_BIG))
    m_old = m_scr[...]
    m_new = m_old
    for s in scores:
        m_new = jnp.maximum(m_new, jnp.max(s, axis=-1, keepdims=True))
    corr = jnp.exp(m_old - m_new)
    l = l_scr[...] * corr
    acc = acc_scr[...] * corr
    for g in range(n_pg):
        pexp = jnp.exp(scores[g] - m_new)
        l = l + jnp.sum(pexp, axis=-1, keepdims=True)
        v2 = v_refs[g][0].reshape(cols, HEAD_W).astype(BF16)
        acc = acc + jnp.dot(pexp.astype(BF16), v2, preferred_element_type=F32)
    m_scr[...] = m_new
    l_scr[...] = l
    acc_scr[...] = acc

    @pl.when(si == ns - 1)
    def _():
        npad = kn_ref.shape[1]
        s = lax.dot_general(qm, kn_ref[0].astype(BF16), nt, preferred_element_type=F32)
        r = lax.broadcasted_iota(jnp.int32, (rows, npad), 0)
        c = lax.broadcasted_iota(jnp.int32, (rows, npad), 1)
        ok = (c < tqn * ATTN_HEADS) & (c % ATTN_HEADS == r // (2 * tqn)) & (c // ATTN_HEADS <= r % tqn)
        s = jnp.where(ok, s, NEG_BIG)
        m1 = jnp.maximum(m_new, jnp.max(s, axis=-1, keepdims=True))
        pexp = jnp.exp(s - m1)
        corr1 = jnp.exp(m_new - m1)
        l1 = l * corr1 + jnp.sum(pexp, axis=-1, keepdims=True)
        acc1 = acc * corr1 + jnp.dot(pexp.astype(BF16), vn_ref[0].astype(BF16), preferred_element_type=F32)
        o = acc1 / l1
        od = o - lam_ref[...] * pltpu.roll(o, rows - tqn, 0)
        ms = jnp.mean(od * od, axis=-1, keepdims=True)
        o_ref[0] = od * lax.rsqrt(ms + RMS_EPS) * sub_ref[...] * out_scale


def _paged_attention(q, k_new, v_new, cache_k, cache_v, page_table, lam, subln, out_scale):
    b, n_pages = page_table.shape
    tqn = q.shape[0] // b
    n_pg = PAGES_PER_STEP
    rows = ATTN_HEADS * 2 * tqn
    q4 = q.reshape(b, tqn, ATTN_HEADS, HEAD_W).transpose(0, 2, 1, 3) * ATTN_SCALE
    lane = jnp.arange(HEAD_W)
    qm = jnp.stack([jnp.where(lane < HALF_DIM, q4, 0.0), jnp.where(lane >= HALF_DIM, q4, 0.0)], axis=2)
    qm = qm.reshape(b, rows, HEAD_W)
    npad = LANES
    kn = jnp.pad(k_new.reshape(b, tqn * ATTN_HEADS, HEAD_W), ((0, 0), (0, npad - tqn * ATTN_HEADS), (0, 0)))
    vn = jnp.pad(v_new.reshape(b, tqn * ATTN_HEADS, HEAD_W), ((0, 0), (0, npad - tqn * ATTN_HEADS), (0, 0)))
    lam_row = jnp.full((1, HEAD_W), lam, F32)

    def page_spec(g):
        return pl.BlockSpec((1, PAGE_SIZE, ATTN_HEADS, HEAD_W),
                            lambda bi, si, pt: (pt[bi, si * n_pg + g], 0, 0, 0))

    per_seq = lambda r: pl.BlockSpec((1, r, HEAD_W), lambda bi, si, pt: (bi, 0, 0))
    grid_spec = pltpu.PrefetchScalarGridSpec(
        num_scalar_prefetch=1,
        grid=(b, n_pages // n_pg),
        in_specs=[pl.BlockSpec((1, HEAD_W), lambda bi, si, pt: (0, 0)),
                  pl.BlockSpec((1, HEAD_W), lambda bi, si, pt: (0, 0)),
                  per_seq(rows), per_seq(npad), per_seq(npad)]
                 + [page_spec(g) for g in range(n_pg)] + [page_spec(g) for g in range(n_pg)],
        out_specs=per_seq(rows),
        scratch_shapes=[pltpu.VMEM((rows, 1), F32), pltpu.VMEM((rows, 1), F32), pltpu.VMEM((rows, HEAD_W), F32)],
    )
    o = pl.pallas_call(
        functools.partial(_paged_kernel, n_pg=n_pg, tqn=tqn, out_scale=out_scale),
        grid_spec=grid_spec,
        out_shape=jax.ShapeDtypeStruct((b, rows, HEAD_W), F32),
        compiler_params=_params("parallel", "arbitrary"),
        name="paged_attention",
    )(page_table, lam_row, subln.reshape(1, HEAD_W), qm, kn, vn, *([cache_k] * n_pg), *([cache_v] * n_pg))
    o = o.reshape(b, ATTN_HEADS, 2, tqn, HEAD_W)[:, :, 0]
    return o.transpose(0, 2, 1, 3).reshape(b * tqn, ATTN_HEADS * HEAD_W).astype(BF16)


def _trunk(x, conv_buf, ssm_state, p, attention):
    b, l, d = x.shape
    t = b * l
    x, new_conv, new_state = _mamba_layer(x, conv_buf, ssm_state, p, 0)
    x = _moe_layer(x.reshape(t, d), p, 0)
    k, v, k_b, v_b = _norm_matmul(x, p['kv_norm'], [p['w_k'].astype(BF16), p['w_v'].astype(BF16)], per_head=True)
    layer = N_A_LAYERS
    lam_init = 0.8 - 0.6 * math.exp(-0.3 * layer)
    lq = p['attn_lambda'][0]
    lam = jnp.exp(jnp.sum(lq[0] * lq[1])) - jnp.exp(jnp.sum(lq[2] * lq[3])) + lam_init
    q, = _norm_matmul(x, p['attn_norm'][0], [p['attn_w_q'][0].astype(BF16)])
    o = attention(q, k, v, k_b, v_b, lam, p['attn_subln'][0], 1.0 - lam_init)
    x = _matmul_res(o, p['attn_w_o'][0].astype(BF16), x)
    y = _moe_layer(x, p, layer, final_gain=p['final_norm'])
    kv_shape = (b, l, ATTN_HEADS, HEAD_W)
    return y.reshape(b, l, d), new_conv[None], new_state[None], k.reshape(kv_shape), v.reshape(kv_shape)


def kernel(x_prompt, x_sample, state_ssm, state_conv, cache_k, cache_v, page_table, ssm_norm, ssm_w_in, ssm_conv_w, ssm_conv_b, ssm_dt_bias, ssm_a_log, ssm_d, ssm_gate_norm, ssm_w_out, kv_norm, w_k, w_v, attn_norm, attn_w_q, attn_lambda, attn_subln, attn_w_o, moe_norm, moe_w_group, moe_w_expert, moe_w_gate, moe_w_up, moe_w_down, final_norm):
    p = dict(ssm_norm=ssm_norm, ssm_w_in=ssm_w_in, ssm_conv_w=ssm_conv_w, ssm_conv_b=ssm_conv_b,
             ssm_dt_bias=ssm_dt_bias, ssm_a_log=ssm_a_log, ssm_d=ssm_d, ssm_gate_norm=ssm_gate_norm,
             ssm_w_out=ssm_w_out, kv_norm=kv_norm, w_k=w_k, w_v=w_v, attn_norm=attn_norm,
             attn_w_q=attn_w_q, attn_lambda=attn_lambda, attn_subln=attn_subln, attn_w_o=attn_w_o,
             moe_norm=moe_norm, moe_w_group=moe_w_group, moe_w_expert=moe_w_expert,
             moe_w_gate=moe_w_gate, moe_w_up=moe_w_up, moe_w_down=moe_w_down, final_norm=final_norm)
    bp, sp, _ = x_prompt.shape
    conv0 = jnp.zeros((bp, CONV_W - 1, CONV_DIM), F32)
    ssm0 = jnp.zeros((bp, SSM_HEADS, SSM_HEAD_DIM, D_STATE), F32)
    def prompt_attn(q, k, v, k_b, v_b, lam, subln, out_scale):
        return _prompt_attention(q, k_b, v_b, lam, subln, out_scale, b=bp, s=sp)

    y_p, conv_p, ssm_p, k_p, v_p = _trunk(x_prompt, conv0, ssm0, p, prompt_attn)

    def sample_attn(q, k, v, k_b, v_b, lam, subln, out_scale):
        return _paged_attention(q, k, v, cache_k, cache_v, page_table, lam, subln, out_scale)

    y_s, conv_s, ssm_s, k_s, v_s = _trunk(x_sample, state_conv[0], state_ssm[0], p, sample_attn)
    return (y_p, y_s, ssm_p, conv_p, k_p, v_p, ssm_s, conv_s, k_s, v_s)
```

```python
import functools
import math

import jax
import jax.numpy as jnp
from jax import lax
from jax.experimental import pallas as pl
from jax.experimental.pallas import tpu as pltpu

F32 = jnp.float32
BF16 = jnp.bfloat16

D_MODEL = 1024
DEPTH = 2
N_A_LAYERS = DEPTH // 2
D_INNER = 2 * D_MODEL
SSM_HEAD_DIM = 64
SSM_HEADS = D_INNER // SSM_HEAD_DIM
SSM_GROUPS = 4
HEADS_PER_GROUP = SSM_HEADS // SSM_GROUPS
D_STATE = 128
CONV_W = 4
CONV_DIM = D_INNER + 2 * SSM_GROUPS * D_STATE
ZX_DIM = D_INNER + CONV_DIM
SSD_CHUNK = 128
GATED_NORM_EPS = 1e-5
HALF_DIM = 64
ATTN_HEADS = D_MODEL // (2 * HALF_DIM)
HEAD_W = 2 * HALF_DIM
ATTN_SCALE = HALF_DIM ** -0.5
N_EGROUPS = 4
EXPERTS_PER_GROUP = 8
N_EXPERTS = N_EGROUPS * EXPERTS_PER_GROUP
TOP_K = 2
EXPERT_FF = D_MODEL // 2
RMS_EPS = 1e-6
PAGE_SIZE = 128

LANES = 128
SUBLANES = 8
MOE_ROWS = 512
MOE_ROWS_SMALL = 128
ATTN_TQ = 512
ATTN_TK = 512
PAGES_PER_STEP = 16
COL_CHUNK = 1024
LOG2E = 1.4426950408889634
NEG_BIG = -1e30
VMEM_LIMIT = 56 * 1024 * 1024


def _params(*sem):
    return pltpu.CompilerParams(dimension_semantics=sem, vmem_limit_bytes=VMEM_LIMIT)


def _silu(x):
    return x * jax.nn.sigmoid(x)


def _split3(x):
    hi = x.astype(BF16)
    r1 = x - hi.astype(F32)
    mid = r1.astype(BF16)
    lo = (r1 - mid.astype(F32)).astype(BF16)
    return jnp.concatenate([hi, mid, lo], axis=1)


def _norm_matmul_kernel(*refs, n_w, has_small, per_head, has_second):
    x_ref, g_ref = refs[0], refs[1]
    w_refs = refs[2:2 + n_w]
    pos = 2 + n_w
    g2_ref, w2_ref = (refs[pos], refs[pos + 1]) if has_second else (None, None)
    pos += 2 * int(has_second)
    ws_ref = refs[pos] if has_small else None
    pos += int(has_small)
    o_refs = refs[pos:pos + n_w]
    pos += n_w
    os_ref = refs[pos] if has_small else None
    pos += int(has_small)
    ob_refs = refs[pos:pos + n_w] if per_head else None
    pos += n_w * int(per_head)
    o2_ref = refs[pos] if has_second else None

    x = x_ref[...]
    ms = jnp.mean(x * x, axis=-1, keepdims=True)
    xhat = x * lax.rsqrt(ms + RMS_EPS)
    h = (xhat * g_ref[...]).astype(BF16)
    if has_second:
        o2_ref[...] = jnp.dot((xhat * g2_ref[...]).astype(BF16), w2_ref[...], preferred_element_type=F32)
    if has_small:
        os_ref[...] = jnp.dot(h, ws_ref[...], preferred_element_type=F32)
    for k, (w_ref, o_ref) in enumerate(zip(w_refs, o_refs)):
        n = w_ref.shape[1]
        step = min(COL_CHUNK, n)
        for c in range(0, n, step):
            res = jnp.dot(h, w_ref[:, c:c + step], preferred_element_type=F32)
            if per_head:
                for hh in range(step // HEAD_W):
                    o_ref[:, c // HEAD_W + hh, :] = res[:, hh * HEAD_W:(hh + 1) * HEAD_W]
                ob_refs[k][:, c:c + step] = res.astype(BF16)
            else:
                o_ref[:, c:c + step] = res


def _resident(shape):
    return pl.BlockSpec(shape, lambda i: (0,) * len(shape), pipeline_mode=pl.Buffered(1))


def _norm_matmul(x, gain, weights, w_small=None, per_head=False, second=None, tm=512):
    t, d = x.shape
    n = weights[0].shape[1]
    tm = min(tm, t)
    n_w = len(weights)
    has_small = w_small is not None
    in_specs = [pl.BlockSpec((tm, d), lambda i: (i, 0)), _resident((1, d))]
    in_specs += [_resident((d, n))] * n_w
    if per_head:
        out_shape = [jax.ShapeDtypeStruct((t, n // HEAD_W, HEAD_W), F32)] * n_w
        out_specs = [pl.BlockSpec((tm, n // HEAD_W, HEAD_W), lambda i: (i, 0, 0))] * n_w
    else:
        out_shape = [jax.ShapeDtypeStruct((t, n), F32)] * n_w
        out_specs = [pl.BlockSpec((tm, n), lambda i: (i, 0))] * n_w
    args = [x, gain.reshape(1, d)] + list(weights)
    has_second = second is not None
    if has_second:
        n2 = second[1].shape[1]
        in_specs += [_resident((1, d)), _resident((d, n2))]
        args += [second[0].reshape(1, d), second[1]]
    if has_small:
        in_specs.append(_resident((d, LANES)))
        out_shape.append(jax.ShapeDtypeStruct((t, LANES), F32))
        out_specs.append(pl.BlockSpec((tm, LANES), lambda i: (i, 0)))
        args.append(w_small)
    if per_head:
        out_shape += [jax.ShapeDtypeStruct((t, n), BF16)] * n_w
        out_specs += [pl.BlockSpec((tm, n), lambda i: (i, 0))] * n_w
    if has_second:
        out_shape.append(jax.ShapeDtypeStruct((t, n2), F32))
        out_specs.append(pl.BlockSpec((tm, n2), lambda i: (i, 0)))
    return pl.pallas_call(
        functools.partial(_norm_matmul_kernel, n_w=n_w, has_small=has_small, per_head=per_head,
                          has_second=has_second),
        grid=(t // tm,),
        in_specs=in_specs, out_specs=out_specs, out_shape=out_shape,
        compiler_params=_params("parallel"),
        name="norm_matmul",
    )(*args)


def _matmul_res_kernel(a_ref, w_ref, r_ref, o_ref):
    o_ref[...] = r_ref[...] + jnp.dot(a_ref[...], w_ref[...], preferred_element_type=F32)


def _matmul_res(a, w, res, tm=512):
    t, k = a.shape
    n = w.shape[1]
    tm = min(tm, t)
    return pl.pallas_call(
        _matmul_res_kernel,
        grid=(t // tm,),
        in_specs=[pl.BlockSpec((tm, k), lambda i: (i, 0)), _resident((k, n)),
                  pl.BlockSpec((tm, n), lambda i: (i, 0))],
        out_specs=pl.BlockSpec((tm, n), lambda i: (i, 0)),
        out_shape=jax.ShapeDtypeStruct((t, n), F32),
        compiler_params=_params("parallel"),
        name="matmul_res",
    )(a, w, res)


def _ssd_kernel(zx_ref, dt_ref, cbuf_ref, h0_ref, convw_ref, convb_ref, dtb_ref, alog_ref, dfull_ref,
                gnorm_ref, tri_ref, e_ref, e2_ref, g_ref, ht_ref, state_scr, tail_scr, *, tc, valid):
    ci = pl.program_id(1)
    nc = pl.num_programs(1)
    gw = HEADS_PER_GROUP * SSM_HEAD_DIM
    n_pairs = HEADS_PER_GROUP // 2

    @pl.when(ci == 0)
    def _init():
        tail_scr[...] = cbuf_ref[0]
        for g in range(SSM_GROUPS):
            for j in range(n_pairs):
                h2 = h0_ref[0, pl.ds(g * HEADS_PER_GROUP + 2 * j, 2)]
                state_scr[g, :, j * LANES:(j + 1) * LANES] = h2.reshape(2 * SSM_HEAD_DIM, D_STATE).T

    xr = zx_ref[0, :, D_INNER:ZX_DIM]
    xcat = jnp.concatenate([tail_scr[...], xr], axis=0)
    conv = convb_ref[...] + convw_ref[CONV_W - 1:CONV_W, :] * xr
    for k in range(1, CONV_W):
        conv = conv + convw_ref[CONV_W - 1 - k:CONV_W - k, :] * xcat[SUBLANES - k:SUBLANES - k + tc]
    tail_scr[...] = xr[tc - SUBLANES:tc]
    xbc = _silu(conv)
    xs = xbc[:, :D_INNER]
    bm = xbc[:, D_INNER:D_INNER + SSM_GROUPS * D_STATE]
    cm = xbc[:, D_INNER + SSM_GROUPS * D_STATE:]

    dtl = dt_ref[0] + dtb_ref[...]
    dt = jnp.maximum(dtl, 0.0) + jnp.log1p(jnp.exp(-jnp.abs(dtl)))
    row = lax.broadcasted_iota(jnp.int32, (tc, LANES), 0)
    if valid is not None:
        dt = jnp.where(row < valid, dt, 0.0)
    la = dt * (-jnp.exp(alog_ref[...]))
    cs3 = jnp.dot(tri_ref[...], _split3(la), preferred_element_type=F32)
    cs = cs3[:, :LANES] + cs3[:, LANES:2 * LANES] + cs3[:, 2 * LANES:]
    cs_last = cs[tc - 1:tc, :]
    ecs = jnp.exp(cs)
    to_end = jnp.exp(cs_last - cs)
    cdecay = jnp.exp(jnp.broadcast_to(cs_last, (SUBLANES, LANES)))

    lane_id = lax.broadcasted_iota(jnp.int32, (tc, LANES), 1)

    def expand(v, e):
        hi = v.astype(BF16).astype(F32)
        r1 = v - hi
        mid = r1.astype(BF16).astype(F32)
        lo = r1 - mid
        lanes = lane_id[:v.shape[0]]
        packed = jnp.where(lanes < SSM_HEADS, hi,
                           jnp.where(lanes < 2 * SSM_HEADS, pltpu.roll(mid, SSM_HEADS, 1),
                                     pltpu.roll(lo, 2 * SSM_HEADS, 1)))
        return jnp.dot(packed.astype(BF16), e, preferred_element_type=F32)

    e_mat = e_ref[...]
    dt_full = expand(dt, e_mat)
    ecs_full = expand(ecs, e_mat)
    toend_full = expand(to_end, e_mat)
    cdecay_full = expand(cdecay, e_mat)[0:1, :]
    cs_col = expand(cs, e2_ref[...])
    cs_row = cs.T

    xdt = xs * dt_full
    xdt_b = xdt.astype(BF16)
    xw_b = (xdt * toend_full).astype(BF16)
    z = zx_ref[0, :, :D_INNER]

    tri_mask = lax.broadcasted_iota(jnp.int32, (tc, tc), 0) >= lax.broadcasted_iota(jnp.int32, (tc, tc), 1)
    lane_lo = lax.broadcasted_iota(jnp.int32, (tc, LANES), 1) < SSM_HEAD_DIM

    for g in range(SSM_GROUPS):
        gs = slice(g * gw, (g + 1) * gw)
        b_g = bm[:, g * D_STATE:(g + 1) * D_STATE]
        c_g = cm[:, g * D_STATE:(g + 1) * D_STATE].astype(BF16)
        cb = lax.dot_general(c_g, b_g.astype(BF16), (((1,), (1,)), ((), ())), preferred_element_type=F32)
        bt_g = b_g.T.astype(BF16)
        st_old = state_scr[g]
        y_off = jnp.dot(c_g, st_old.astype(BF16), preferred_element_type=F32)
        s_new = jnp.dot(bt_g, xw_b[:, gs], preferred_element_type=F32)
        state_scr[g] = st_old * cdecay_full[:, gs] + s_new
        pairs = []
        for j in range(n_pairs):
            x_pair = xdt_b[:, g * gw + j * LANES:g * gw + (j + 1) * LANES]
            acc = None
            for r2 in range(2):
                h = g * HEADS_PER_GROUP + 2 * j + r2
                seg = cs_col[:, h * LANES:(h + 1) * LANES] - cs_row[h:h + 1, :]
                m = (jnp.exp(jnp.where(tri_mask, seg, NEG_BIG)) * cb).astype(BF16)
                keep = lane_lo if r2 == 0 else jnp.logical_not(lane_lo)
                rhs = jnp.where(keep, x_pair, jnp.zeros_like(x_pair))
                part = jnp.dot(m, rhs, preferred_element_type=F32)
                acc = part if acc is None else acc + part
            pairs.append(acc)
        y = jnp.concatenate(pairs, axis=1) + y_off * ecs_full[:, gs] + xs[:, gs] * dfull_ref[:, gs]
        gated = y * _silu(z[:, gs])
        ms = jnp.mean(gated * gated, axis=-1, keepdims=True)
        g_ref[0, :, gs] = (gated * lax.rsqrt(ms + GATED_NORM_EPS) * gnorm_ref[:, gs]).astype(g_ref.dtype)

    @pl.when(ci == nc - 1)
    def _fin():
        for g in range(SSM_GROUPS):
            for j in range(n_pairs):
                st = state_scr[g, :, j * LANES:(j + 1) * LANES].T
                ht_ref[0, pl.ds(g * HEADS_PER_GROUP + 2 * j, 2)] = st.reshape(2, SSM_HEAD_DIM, D_STATE)


def _ssd_constants(tc):
    tri = (jnp.arange(tc)[:, None] >= jnp.arange(tc)[None, :]).astype(BF16)
    head = jnp.arange(LANES)

    def expansion(width):
        cols = jnp.arange(SSM_HEADS * width) // width
        piece = head % SSM_HEADS
        return ((piece[:, None] == cols[None, :]) & (head[:, None] < 3 * SSM_HEADS)).astype(BF16)

    return tri, expansion(SSM_HEAD_DIM), expansion(LANES)


def _ssd(zx, dtr, cbuf8, h0, conv_w, conv_b, dt_bias, a_log, d_skip, gate_norm, valid):
    b, lp, _ = zx.shape
    tc = SSD_CHUNK
    nc = lp // tc
    tri, e_mat, e2_mat = _ssd_constants(tc)
    pad = LANES - SSM_HEADS
    dtb = jnp.pad(dt_bias, (0, pad)).reshape(1, LANES)
    alog = jnp.pad(a_log, (0, pad)).reshape(1, LANES)
    dfull = jnp.repeat(d_skip, SSM_HEAD_DIM).reshape(1, D_INNER)
    const = lambda *shape: pl.BlockSpec(shape, lambda bi, ci: (0,) * len(shape))
    return pl.pallas_call(
        functools.partial(_ssd_kernel, tc=tc, valid=valid),
        grid=(b, nc),
        in_specs=[pl.BlockSpec((1, tc, ZX_DIM), lambda bi, ci: (bi, ci, 0)),
                  pl.BlockSpec((1, tc, LANES), lambda bi, ci: (bi, ci, 0)),
                  pl.BlockSpec((1, SUBLANES, CONV_DIM), lambda bi, ci: (bi, 0, 0)),
                  pl.BlockSpec((1, SSM_HEADS, SSM_HEAD_DIM, D_STATE), lambda bi, ci: (bi, 0, 0, 0)),
                  const(CONV_W, CONV_DIM), const(1, CONV_DIM), const(1, LANES), const(1, LANES),
                  const(1, D_INNER), const(1, D_INNER),
                  const(tc, tc), const(LANES, D_INNER), const(LANES, SSM_HEADS * LANES)],
        out_specs=[pl.BlockSpec((1, tc, D_INNER), lambda bi, ci: (bi, ci, 0)),
                   pl.BlockSpec((1, SSM_HEADS, SSM_HEAD_DIM, D_STATE), lambda bi, ci: (bi, 0, 0, 0))],
        out_shape=[jax.ShapeDtypeStruct((b, lp, D_INNER), BF16),
                   jax.ShapeDtypeStruct((b, SSM_HEADS, SSM_HEAD_DIM, D_STATE), F32)],
        scratch_shapes=[pltpu.VMEM((SSM_GROUPS, D_STATE, HEADS_PER_GROUP * SSM_HEAD_DIM), F32),
                        pltpu.VMEM((SUBLANES, CONV_DIM), F32)],
        compiler_params=_params("parallel", "arbitrary"),
        name="ssd",
    )(zx, dtr, cbuf8, h0, conv_w, conv_b.reshape(1, CONV_DIM), dtb, alog, dfull,
      gate_norm.reshape(1, D_INNER), tri, e_mat, e2_mat)


def _mamba_layer(x, conv_buf, ssm_state, p, j):
    b, l, d = x.shape
    t = b * l
    w_in = p['ssm_w_in'][j]
    w_zx = w_in[:, :ZX_DIM].astype(BF16)
    w_dt = jnp.pad(w_in[:, ZX_DIM:], ((0, 0), (0, LANES - SSM_HEADS))).astype(BF16)
    zx, dtr = _norm_matmul(x.reshape(t, d), p['ssm_norm'][j], [w_zx], w_small=w_dt)
    zx = zx.reshape(b, l, ZX_DIM)
    dtr = dtr.reshape(b, l, LANES)
    xbc_raw = zx[:, :, D_INNER:]
    new_conv = jnp.concatenate([conv_buf, xbc_raw[:, max(l - (CONV_W - 1), 0):]], axis=1)[:, -(CONV_W - 1):]
    lp = -(-l // SSD_CHUNK) * SSD_CHUNK
    valid = None
    if lp != l:
        valid = l
        zx = jnp.pad(zx, ((0, 0), (0, lp - l), (0, 0)))
        dtr = jnp.pad(dtr, ((0, 0), (0, lp - l), (0, 0)))
    cbuf8 = jnp.pad(conv_buf, ((0, 0), (SUBLANES - (CONV_W - 1), 0), (0, 0)))
    gated, new_state = _ssd(zx, dtr, cbuf8, ssm_state, p['ssm_conv_w'][j], p['ssm_conv_b'][j],
                            p['ssm_dt_bias'][j], p['ssm_a_log'][j], p['ssm_d'][j], p['ssm_gate_norm'][j], valid)
    gated = gated[:, :l].reshape(t, D_INNER)
    x_new = _matmul_res(gated, p['ssm_w_out'][j].astype(BF16), x.reshape(t, d))
    return x_new.reshape(b, l, d), new_conv, new_state


def _router_kernel(x_ref, g_ref, wr_ref, low_ref, h_ref, route_ref, cnt_ref, route_t_ref, carry_scr):
    @pl.when(pl.program_id(0) == 0)
    def _():
        carry_scr[...] = jnp.zeros_like(carry_scr)

    x = x_ref[...]
    ms = jnp.mean(x * x, axis=-1, keepdims=True)
    h = x * lax.rsqrt(ms + RMS_EPS) * g_ref[...]
    h1 = h.astype(BF16)
    h_ref[...] = h1
    h2 = (h - h1.astype(F32)).astype(BF16)
    logits = jnp.dot(jnp.concatenate([h1, h1, h2], axis=1), wr_ref[...], preferred_element_type=F32)
    tm = x.shape[0]
    lane = lax.broadcasted_iota(jnp.int32, (tm, LANES), 1)
    gl = jnp.where(lane < N_EGROUPS, logits, NEG_BIG)
    gmax = jnp.max(gl, axis=-1, keepdims=True)
    g_sel = jnp.min(jnp.where(gl == gmax, lane, LANES), axis=-1, keepdims=True)
    p_group = 1.0 / jnp.sum(jnp.exp(gl - gmax), axis=-1, keepdims=True)
    lo = N_EGROUPS + EXPERTS_PER_GROUP * g_sel
    el = jnp.where((lane >= lo) & (lane < lo + EXPERTS_PER_GROUP), logits, NEG_BIG)
    v1 = jnp.max(el, axis=-1, keepdims=True)
    i1 = jnp.min(jnp.where(el == v1, lane, LANES), axis=-1, keepdims=True)
    el2 = jnp.where(lane == i1, NEG_BIG, el)
    v2 = jnp.max(el2, axis=-1, keepdims=True)
    i2 = jnp.min(jnp.where(el2 == v2, lane, LANES), axis=-1, keepdims=True)
    e2 = jnp.exp(v2 - v1)
    gate1 = p_group / (1.0 + e2)
    gate2 = p_group * e2 / (1.0 + e2)
    hit1 = lane == i1
    hit2 = lane == i2
    sel = jnp.where(hit1 | hit2, 1.0, 0.0)
    before = jnp.dot(low_ref[...], sel.astype(BF16), preferred_element_type=F32) + carry_scr[0:1, :]
    rank1 = jnp.sum(jnp.where(hit1, before, 0.0), axis=-1, keepdims=True)
    rank2 = jnp.sum(jnp.where(hit2, before, 0.0), axis=-1, keepdims=True)
    carry_scr[...] = carry_scr[...] + jnp.sum(sel, axis=0, keepdims=True)
    cnt_ref[...] = carry_scr[...]
    vals = ((i1 - N_EGROUPS).astype(F32), (i2 - N_EGROUPS).astype(F32), gate1, gate2, rank1, rank2)
    out = jnp.zeros((tm, LANES), F32)
    for k, v in enumerate(vals):
        out = jnp.where(lane == k, v, out)
    route_ref[...] = out
    route_t_ref[...] = out.T[:SUBLANES, :]


def _router(x, gain, w_group, w_expert, layer, tm=512):
    t, d = x.shape
    tm = min(tm, t)
    wr = jnp.concatenate([w_group, w_expert], axis=2)
    wr = jnp.pad(wr, ((0, 0), (0, 0), (0, LANES - wr.shape[2])))
    w1 = wr.astype(BF16)
    w2 = (wr - w1.astype(F32)).astype(BF16)
    wr3 = jnp.concatenate([w1, w2, w1], axis=1)[layer]
    strict_lower = (jnp.arange(tm)[:, None] > jnp.arange(tm)[None, :]).astype(BF16)
    return pl.pallas_call(
        _router_kernel,
        grid=(t // tm,),
        in_specs=[pl.BlockSpec((tm, d), lambda i: (i, 0)), _resident((1, d)), _resident((3 * d, LANES)),
                  _resident((tm, tm))],
        out_specs=[pl.BlockSpec((tm, d), lambda i: (i, 0)),
                   pl.BlockSpec((tm, LANES), lambda i: (i, 0)),
                   pl.BlockSpec((SUBLANES, LANES), lambda i: (0, 0)),
                   pl.BlockSpec((SUBLANES, tm), lambda i: (0, i))],
        out_shape=[jax.ShapeDtypeStruct((t, d), BF16), jax.ShapeDtypeStruct((t, LANES), F32),
                   jax.ShapeDtypeStruct((SUBLANES, LANES), F32), jax.ShapeDtypeStruct((SUBLANES, t), F32)],
        scratch_shapes=[pltpu.VMEM((SUBLANES, LANES), F32)],
        compiler_params=_params("arbitrary"),
        name="moe_router",
    )(x, gain.reshape(1, d), wr3, strict_lower)


def _grouped_mlp_kernel(be_ref, nu_ref, xs_ref, wg_ref, wu_ref, wd_ref, o_ref, wg_s, wu_s, wd_s):
    b = pl.program_id(0)
    changed = jnp.logical_or(b == 0, be_ref[b] != be_ref[jnp.maximum(b - 1, 0)])

    @pl.when(changed)
    def _():
        wg_s[...] = wg_ref[0, 0].astype(BF16)
        wu_s[...] = wu_ref[0, 0].astype(BF16)
        wd_s[...] = wd_ref[0, 0].astype(BF16)

    @pl.when(b < nu_ref[0])
    def _():
        x = xs_ref[...]
        gate = jnp.dot(x, wg_s[...], preferred_element_type=F32)
        up = jnp.dot(x, wu_s[...], preferred_element_type=F32)
        hid = (_silu(gate) * up).astype(BF16)
        o_ref[...] = jnp.dot(hid, wd_s[...], preferred_element_type=F32).astype(o_ref.dtype)

    @pl.when(b >= nu_ref[0])
    def _():
        o_ref[...] = jnp.zeros_like(o_ref)


def _grouped_mlp(xs, block_expert, n_used, w_gate, w_up, w_down, layer, rows):
    n_slots, d = xs.shape
    ff = w_gate.shape[3]
    n_blocks = n_slots // rows
    grid_spec = pltpu.PrefetchScalarGridSpec(
        num_scalar_prefetch=2,
        grid=(n_blocks,),
        in_specs=[pl.BlockSpec((rows, d), lambda b, be, nu: (b, 0)),
                  pl.BlockSpec((1, 1, d, ff), lambda b, be, nu: (layer, be[b], 0, 0)),
                  pl.BlockSpec((1, 1, d, ff), lambda b, be, nu: (layer, be[b], 0, 0)),
                  pl.BlockSpec((1, 1, ff, d), lambda b, be, nu: (layer, be[b], 0, 0))],
        out_specs=pl.BlockSpec((rows, d), lambda b, be, nu: (b, 0)),
        scratch_shapes=[pltpu.VMEM((d, ff), BF16), pltpu.VMEM((d, ff), BF16), pltpu.VMEM((ff, d), BF16)],
    )
    return pl.pallas_call(
        _grouped_mlp_kernel,
        grid_spec=grid_spec,
        out_shape=jax.ShapeDtypeStruct((n_slots, d), BF16),
        compiler_params=_params("arbitrary"),
        name="moe_experts",
    )(block_expert, n_used, xs, w_gate, w_up, w_down)


def _combine_kernel(x_ref, y0_ref, y1_ref, route_ref, gain_ref, o_ref, *, final):
    r = route_ref[...]
    xn = x_ref[...] + r[:, 2:3] * y0_ref[...].astype(F32) + r[:, 3:4] * y1_ref[...].astype(F32)
    if final:
        ms = jnp.mean(xn * xn, axis=-1, keepdims=True)
        xn = xn * lax.rsqrt(ms + RMS_EPS) * gain_ref[...]
    o_ref[...] = xn


def _combine(x, y01, route, final_gain, tm=512):
    t, d = x.shape
    tm = min(tm, t)
    final = final_gain is not None
    gain = (final_gain if final else jnp.ones((d,), F32)).reshape(1, d)
    row = pl.BlockSpec((tm, d), lambda i: (i, 0))
    return pl.pallas_call(
        functools.partial(_combine_kernel, final=final),
        grid=(t // tm,),
        in_specs=[row, row, pl.BlockSpec((tm, d), lambda i: (i + t // tm, 0)),
                  pl.BlockSpec((tm, LANES), lambda i: (i, 0)), _resident((1, d))],
        out_specs=row,
        out_shape=jax.ShapeDtypeStruct((t, d), F32),
        compiler_params=_params("parallel"),
        name="moe_combine",
    )(x, y01, y01, route, gain)


def _moe_layer(x, p, i, final_gain=None):
    t, d = x.shape
    hb, route, cnt, route_t = _router(x, p['moe_norm'][i], p['moe_w_group'], p['moe_w_expert'], i)
    a = t * TOP_K
    r = MOE_ROWS if a >= N_EXPERTS * MOE_ROWS else MOE_ROWS_SMALL
    i32 = jnp.int32
    e_idx = route_t[:TOP_K].astype(i32)
    rank = route_t[4:4 + TOP_K].astype(i32)
    counts = cnt[0, N_EGROUPS:N_EGROUPS + N_EXPERTS].astype(i32)
    padded = (counts + r - 1) // r * r
    pad_end = jnp.cumsum(padded)
    pad_start = pad_end - padded
    start = jnp.cumsum(counts) - counts
    experts = jnp.arange(N_EXPERTS, dtype=i32)
    seg_start = jnp.sum(jnp.where(e_idx[:, None, :] == experts[None, :, None], pad_start[None, :, None], 0), axis=1)
    dest = (seg_start + rank).reshape(a)
    n_blocks = (a + N_EXPERTS * (r - 1)) // r
    n_used = (pad_end[-1] // r).astype(i32)
    blk = jnp.arange(n_blocks, dtype=i32)
    block_expert = jnp.minimum(jnp.searchsorted(pad_end, blk * r, side='right'), N_EXPERTS - 1).astype(i32)
    order = jnp.argsort(e_idx.T.reshape(a), stable=True).astype(i32)
    off = blk[:, None] * r + jnp.arange(r, dtype=i32)[None, :] - pad_start[block_expert][:, None]
    live = (off < counts[block_expert][:, None]) & (blk < n_used)[:, None]
    src = jnp.clip(start[block_expert][:, None] + off, 0, a - 1)
    spread = (blk[:, None] * r + jnp.arange(r, dtype=i32)[None, :]) % t
    slot_tok = jnp.where(live, order[src.reshape(-1)].reshape(n_blocks, r) // TOP_K, spread).reshape(n_blocks * r)
    last_e = block_expert[jnp.maximum(n_used - 1, 0)]
    block_expert = jnp.where(blk < n_used, block_expert, last_e)
    xs = hb[slot_tok]
    yb = _grouped_mlp(xs, block_expert, n_used.reshape(1), p['moe_w_gate'], p['moe_w_up'], p['moe_w_down'], i, r)
    return _combine(x, yb[dest], route, final_gain)


def _attn_finish(o_halves, tq, lam, subln, out_scale):
    od = o_halves[:tq] - lam * o_halves[tq:]
    ms = jnp.mean(od * od, axis=-1, keepdims=True)
    return od * lax.rsqrt(ms + RMS_EPS) * subln * out_scale


def _attn_kernel(lam_ref, sub_ref, q_ref, k_ref, v_ref, o_ref, vb_scr, *, tq, tk, out_scale):
    qi = pl.program_id(2)

    @pl.when(qi == 0)
    def _():
        vb_scr[:, :HEAD_W] = v_ref[0]
        vb_scr[:, HEAD_W:] = jnp.ones((vb_scr.shape[0], HEAD_W), BF16)

    q = q_ref[0] * (ATTN_SCALE * LOG2E)
    lane = lax.broadcasted_iota(jnp.int32, (tq, HEAD_W), 1)
    qs = jnp.concatenate([jnp.where(lane < HALF_DIM, q, 0.0), jnp.where(lane >= HALF_DIM, q, 0.0)],
                         axis=0).astype(BF16)

    def scores(j):
        kblk = k_ref[0, pl.ds(pl.multiple_of(j * tk, tk), tk), :]
        return lax.dot_general(qs, kblk, (((1,), (1,)), ((), ())), preferred_element_type=F32)

    def softmax_pv(j, s, m, acc, masked):
        vblk = vb_scr[pl.ds(pl.multiple_of(j * tk, tk), tk), :]
        if masked:
            rq = lax.broadcasted_iota(jnp.int32, (2 * tq, tk), 0)
            rq = jnp.where(rq >= tq, rq - tq, rq) + qi * tq
            ck = lax.broadcasted_iota(jnp.int32, (2 * tq, tk), 1) + j * tk
            s = jnp.where(ck <= rq, s, NEG_BIG)
        m_new = jnp.maximum(m, jnp.max(s, axis=-1, keepdims=True))
        pexp = jnp.exp2((s - m_new).astype(BF16))
        acc = acc * jnp.exp2(m - m_new) + jnp.dot(pexp, vblk, preferred_element_type=F32)
        return m_new, acc

    def body(j, carry):
        s, m, acc = carry
        s_next = scores(j + 1)
        m, acc = softmax_pv(j, s, m, acc, False)
        return s_next, m, acc

    init = (scores(0), jnp.full((2 * tq, 1), NEG_BIG, F32), jnp.zeros((2 * tq, 2 * HEAD_W), F32))
    n_full = (qi * tq) // tk
    s, m, acc = lax.fori_loop(0, n_full, body, init)
    m, acc = softmax_pv(n_full, s, m, acc, True)
    o = acc[:, :HEAD_W] / acc[:, HEAD_W:]
    o_ref[0] = _attn_finish(o, tq, lam_ref[...], sub_ref[...], out_scale).astype(o_ref.dtype)


def _prompt_attention(q, k, v, lam, subln, out_scale, b, s):
    d = q.shape[1]
    tq = min(ATTN_TQ, s)
    tk = min(ATTN_TK, s)
    q3, k3, v3 = (a.reshape(b, s, d) for a in (q, k, v))
    lam_row = jnp.full((1, HEAD_W), lam, F32)
    out = pl.pallas_call(
        functools.partial(_attn_kernel, tq=tq, tk=tk, out_scale=out_scale),
        grid=(b, ATTN_HEADS, s // tq),
        in_specs=[pl.BlockSpec((1, HEAD_W), lambda bi, h, qi: (0, 0)),
                  pl.BlockSpec((1, HEAD_W), lambda bi, h, qi: (0, 0)),
                  pl.BlockSpec((1, tq, HEAD_W), lambda bi, h, qi: (bi, qi, h)),
                  pl.BlockSpec((1, s, HEAD_W), lambda bi, h, qi: (bi, 0, h)),
                  pl.BlockSpec((1, s, HEAD_W), lambda bi, h, qi: (bi, 0, h))],
        out_specs=pl.BlockSpec((1, tq, HEAD_W), lambda bi, h, qi: (bi, qi, h)),
        out_shape=jax.ShapeDtypeStruct((b, s, d), BF16),
        scratch_shapes=[pltpu.VMEM((s, 2 * HEAD_W), BF16)],
        compiler_params=_params("parallel", "parallel", "arbitrary"),
        name="prompt_attention",
    )(lam_row, subln.reshape(1, HEAD_W), q3, k3, v3)
    return out.reshape(b * s, d)


def _paged_kernel(pt_ref, lam_ref, sub_ref, qm_ref, kn_ref, vn_ref, *rest, n_pg, tqn, out_scale):
    k_refs = rest[:n_pg]
    v_refs = rest[n_pg:2 * n_pg]
    o_ref = rest[2 * n_pg]
    m_scr, l_scr, acc_scr = rest[2 * n_pg + 1:]
    si = pl.program_id(1)
    ns = pl.num_programs(1)
    rows = ATTN_HEADS * 2 * tqn
    cols = PAGE_SIZE * ATTN_HEADS

    @pl.when(si == 0)
    def _():
        m_scr[...] = jnp.full_like(m_scr, NEG_BIG)
        l_scr[...] = jnp.zeros_like(l_scr)
        acc_scr[...] = jnp.zeros_like(acc_scr)

    qm = qm_ref[0].astype(BF16)
    nt = (((1,), (1,)), ((), ()))
    same_head = (lax.broadcasted_iota(jnp.int32, (rows, cols), 0) // (2 * tqn)
                 == lax.broadcasted_iota(jnp.int32, (rows, cols), 1) % ATTN_HEADS)
    scores = []
    for g in range(n_pg):
        k2 = k_refs[g][0].reshape(cols, HEAD_W).astype(BF16)
        s = lax.dot_general(qm, k2, nt, preferred_element_type=F32)
        scores.append(jnp.where(same_head, s, NEG_BIG))
    m_old = m_scr[...]
    m_new = m_old
    for s in scores:
        m_new = jnp.maximum(m_new, jnp.max(s, axis=-1, keepdims=True))
    corr = jnp.exp(m_old - m_new)
    l = l_scr[...] * corr
    acc = acc_scr[...] * corr
    for g in range(n_pg):
        pexp = jnp.exp(scores[g] - m_new)
        l = l + jnp.sum(pexp, axis=-1, keepdims=True)
        v2 = v_refs[g][0].reshape(cols, HEAD_W).astype(BF16)
        acc = acc + jnp.dot(pexp.astype(BF16), v2, preferred_element_type=F32)
    m_scr[...] = m_new
    l_scr[...] = l
    acc_scr[...] = acc

    @pl.when(si == ns - 1)
    def _():
        npad = kn_ref.shape[1]
        s = lax.dot_general(qm, kn_ref[0].astype(BF16), nt, preferred_element_type=F32)
        r = lax.broadcasted_iota(jnp.int32, (rows, npad), 0)
        c = lax.broadcasted_iota(jnp.int32, (rows, npad), 1)
        ok = (c < tqn * ATTN_HEADS) & (c % ATTN_HEADS == r // (2 * tqn)) & (c // ATTN_HEADS <= r % tqn)
        s = jnp.where(ok, s, NEG_BIG)
        m1 = jnp.maximum(m_new, jnp.max(s, axis=-1, keepdims=True))
        pexp = jnp.exp(s - m1)
        corr1 = jnp.exp(m_new - m1)
        l1 = l * corr1 + jnp.sum(pexp, axis=-1, keepdims=True)
        acc1 = acc * corr1 + jnp.dot(pexp.astype(BF16), vn_ref[0].astype(BF16), preferred_element_type=F32)
        o = acc1 / l1
        od = o - lam_ref[...] * pltpu.roll(o, rows - tqn, 0)
        ms = jnp.mean(od * od, axis=-1, keepdims=True)
        o_ref[0] = od * lax.rsqrt(ms + RMS_EPS) * sub_ref[...] * out_scale


def _paged_attention(q, k_new, v_new, cache_k, cache_v, page_table, lam, subln, out_scale):
    b, n_pages = page_table.shape
    tqn = q.shape[0] // b
    n_pg = PAGES_PER_STEP
    rows = ATTN_HEADS * 2 * tqn
    q4 = q.reshape(b, tqn, ATTN_HEADS, HEAD_W).transpose(0, 2, 1, 3) * ATTN_SCALE
    lane = jnp.arange(HEAD_W)
    qm = jnp.stack([jnp.where(lane < HALF_DIM, q4, 0.0), jnp.where(lane >= HALF_DIM, q4, 0.0)], axis=2)
    qm = qm.reshape(b, rows, HEAD_W)
    npad = LANES
    kn = jnp.pad(k_new.reshape(b, tqn * ATTN_HEADS, HEAD_W), ((0, 0), (0, npad - tqn * ATTN_HEADS), (0, 0)))
    vn = jnp.pad(v_new.reshape(b, tqn * ATTN_HEADS, HEAD_W), ((0, 0), (0, npad - tqn * ATTN_HEADS), (0, 0)))
    lam_row = jnp.full((1, HEAD_W), lam, F32)

    def page_spec(g):
        return pl.BlockSpec((1, PAGE_SIZE, ATTN_HEADS, HEAD_W),
                            lambda bi, si, pt: (pt[bi, si * n_pg + g], 0, 0, 0))

    per_seq = lambda r: pl.BlockSpec((1, r, HEAD_W), lambda bi, si, pt: (bi, 0, 0))
    grid_spec = pltpu.PrefetchScalarGridSpec(
        num_scalar_prefetch=1,
        grid=(b, n_pages // n_pg),
        in_specs=[pl.BlockSpec((1, HEAD_W), lambda bi, si, pt: (0, 0)),
                  pl.BlockSpec((1, HEAD_W), lambda bi, si, pt: (0, 0)),
                  per_seq(rows), per_seq(npad), per_seq(npad)]
                 + [page_spec(g) for g in range(n_pg)] + [page_spec(g) for g in range(n_pg)],
        out_specs=per_seq(rows),
        scratch_shapes=[pltpu.VMEM((rows, 1), F32), pltpu.VMEM((rows, 1), F32), pltpu.VMEM((rows, HEAD_W), F32)],
    )
    o = pl.pallas_call(
        functools.partial(_paged_kernel, n_pg=n_pg, tqn=tqn, out_scale=out_scale),
        grid_spec=grid_spec,
        out_shape=jax.ShapeDtypeStruct((b, rows, HEAD_W), F32),
        compiler_params=_params("parallel", "arbitrary"),
        name="paged_attention",
    )(page_table, lam_row, subln.reshape(1, HEAD_W), qm, kn, vn, *([cache_k] * n_pg), *([cache_v] * n_pg))
    o = o.reshape(b, ATTN_HEADS, 2, tqn, HEAD_W)[:, :, 0]
    return o.transpose(0, 2, 1, 3).reshape(b * tqn, ATTN_HEADS * HEAD_W).astype(BF16)


def _trunk(x, conv_buf, ssm_state, p, attention):
    b, l, d = x.shape
    t = b * l
    x, new_conv, new_state = _mamba_layer(x, conv_buf, ssm_state, p, 0)
    x = _moe_layer(x.reshape(t, d), p, 0)
    k, v, k_b, v_b, q = _norm_matmul(x, p['kv_norm'], [p['w_k'].astype(BF16), p['w_v'].astype(BF16)], per_head=True,
                                     second=(p['attn_norm'][0], p['attn_w_q'][0].astype(BF16)))
    layer = N_A_LAYERS
    lam_init = 0.8 - 0.6 * math.exp(-0.3 * layer)
    lq = p['attn_lambda'][0]
    lam = jnp.exp(jnp.sum(lq[0] * lq[1])) - jnp.exp(jnp.sum(lq[2] * lq[3])) + lam_init
    o = attention(q, k, v, k_b, v_b, lam, p['attn_subln'][0], 1.0 - lam_init)
    x = _matmul_res(o, p['attn_w_o'][0].astype(BF16), x)
    y = _moe_layer(x, p, layer, final_gain=p['final_norm'])
    kv_shape = (b, l, ATTN_HEADS, HEAD_W)
    return y.reshape(b, l, d), new_conv[None], new_state[None], k.reshape(kv_shape), v.reshape(kv_shape)


def kernel(x_prompt, x_sample, state_ssm, state_conv, cache_k, cache_v, page_table, ssm_norm, ssm_w_in, ssm_conv_w, ssm_conv_b, ssm_dt_bias, ssm_a_log, ssm_d, ssm_gate_norm, ssm_w_out, kv_norm, w_k, w_v, attn_norm, attn_w_q, attn_lambda, attn_subln, attn_w_o, moe_norm, moe_w_group, moe_w_expert, moe_w_gate, moe_w_up, moe_w_down, final_norm):
    p = dict(ssm_norm=ssm_norm, ssm_w_in=ssm_w_in, ssm_conv_w=ssm_conv_w, ssm_conv_b=ssm_conv_b,
             ssm_dt_bias=ssm_dt_bias, ssm_a_log=ssm_a_log, ssm_d=ssm_d, ssm_gate_norm=ssm_gate_norm,
             ssm_w_out=ssm_w_out, kv_norm=kv_norm, w_k=w_k, w_v=w_v, attn_norm=attn_norm,
             attn_w_q=attn_w_q, attn_lambda=attn_lambda, attn_subln=attn_subln, attn_w_o=attn_w_o,
             moe_norm=moe_norm, moe_w_group=moe_w_group, moe_w_expert=moe_w_expert,
             moe_w_gate=moe_w_gate, moe_w_up=moe_w_up, moe_w_down=moe_w_down, final_norm=final_norm)
    bp, sp, _ = x_prompt.shape
    conv0 = jnp.zeros((bp, CONV_W - 1, CONV_DIM), F32)
    ssm0 = jnp.zeros((bp, SSM_HEADS, SSM_HEAD_DIM, D_STATE), F32)
    def prompt_attn(q, k, v, k_b, v_b, lam, subln, out_scale):
        return _prompt_attention(q, k_b, v_b, lam, subln, out_scale, b=bp, s=sp)

    y_p, conv_p, ssm_p, k_p, v_p = _trunk(x_prompt, conv0, ssm0, p, prompt_attn)

    def sample_attn(q, k, v, k_b, v_b, lam, subln, out_scale):
        return _paged_attention(q, k, v, cache_k, cache_v, page_table, lam, subln, out_scale)

    y_s, conv_s, ssm_s, k_s, v_s = _trunk(x_sample, state_conv[0], state_ssm[0], p, sample_attn)
    return (y_p, y_s, ssm_p, conv_p, k_p, v_p, ssm_s, conv_s, k_s, v_s)
```
